```python
import math
import jax, jax.numpy as jnp
from jax import lax
import numpy as np

D_MODEL = 2048
BATCH = 4
SEQ = 2048
DEPTH = 4

GRID_W = 64
CTX_LEN = 256
N_GROUPS = 4
GROUP_W = D_MODEL // N_GROUPS
MIX_W = N_GROUPS * GROUP_W
CONV_WIDTH = 3

RWKV_HEAD = 64
RWKV_HEADS = GROUP_W // RWKV_HEAD
RWKV_DECAY_LORA = 96
RWKV_ICLR_LORA = 96
RWKV_GATE_LORA = 256
RWKV_IN = 3 * GROUP_W + RWKV_DECAY_LORA + RWKV_ICLR_LORA + RWKV_GATE_LORA
RWKV_SPLITS = [GROUP_W, 2 * GROUP_W, 3 * GROUP_W, 3 * GROUP_W + RWKV_DECAY_LORA,
               3 * GROUP_W + RWKV_DECAY_LORA + RWKV_ICLR_LORA]
RWKV_DECAY_SCALE = math.exp(-0.5)
RWKV_GN_EPS = 64e-5

MLA_HEADS = 4
MLA_NOPE = 128
MLA_ROPE = 64
MLA_V = GROUP_W // MLA_HEADS
MLA_Q_RANK = 384
MLA_KV_RANK = 256
MLA_IN = MLA_Q_RANK + MLA_KV_RANK + MLA_ROPE
MLA_SCALE = (MLA_NOPE + MLA_ROPE) ** -0.5
Q_BLOCK = 128

CONV_GROUPS = 8
CONV_IN = 3 * GROUP_W

RET_HEADS = 4
RET_HEAD = GROUP_W // RET_HEADS
RET_CHUNK = 128
RET_IN = 4 * GROUP_W
GN_EPS = 1e-5

IN_W = RWKV_IN + MLA_IN + CONV_IN + RET_IN
GROUP_SPLITS = [RWKV_IN, RWKV_IN + MLA_IN, RWKV_IN + MLA_IN + CONV_IN]

D_FF = 5632
N_MOD = 6
ROPE_BASE = 10000.0
NORM_EPS = 1e-6

kernel_name = 'hybrid_parallel_groups_dit_block'

f32 = jnp.float32


def rms_norm(x, g):
    xf = x.astype(f32)
    y = xf * lax.rsqrt(jnp.mean(xf * xf, axis=-1, keepdims=True) + NORM_EPS)
    return (y * g.astype(f32)).astype(x.dtype)


def head_norm(y, eps):
    yf = y.astype(f32)
    mu = jnp.mean(yf, axis=-1, keepdims=True)
    var = jnp.mean(jnp.square(yf - mu), axis=-1, keepdims=True)
    return (yf - mu) * lax.rsqrt(var + eps)


def dwconv3(u, w):
    ch = u.shape[-1]
    return lax.conv_general_dilated(u, w[:, None, :].astype(u.dtype), window_strides=(1,),
                                    padding=((1, 1),), dimension_numbers=('NWC', 'WIO', 'NWC'),
                                    feature_group_count=ch)


def token_shift(u, mu_prev, mu_next):
    prev = jnp.pad(u, ((0, 0), (1, 0), (0, 0)))[:, :-1]
    nxt = jnp.pad(u, ((0, 0), (0, 1), (0, 0)))[:, 1:]
    return u + mu_prev * (prev - u) + mu_next * (nxt - u)


def axial_rope_angles(rows, rot_dim):
    half = rot_dim // 2
    inv = ROPE_BASE ** (-jnp.arange(0, half, 2, dtype=f32) / half)
    row = jnp.repeat(jnp.arange(rows, dtype=f32), GRID_W)
    col = jnp.tile(jnp.arange(GRID_W, dtype=f32), rows)
    return row[:, None] * inv[None, :], col[:, None] * inv[None, :]


def rope_rotate(x, ang):
    x1, x2 = jnp.split(x, 2, axis=-1)
    cos = jnp.cos(ang)[None, :, None, :].astype(x.dtype)
    sin = jnp.sin(ang)[None, :, None, :].astype(x.dtype)
    return jnp.concatenate([x1 * cos - x2 * sin, x2 * cos + x1 * sin], axis=-1)


def axial_rope(x, angs):
    d = x.shape[-1] // 2
    return jnp.concatenate([rope_rotate(x[..., :d], angs[0]), rope_rotate(x[..., d:], angs[1])], axis=-1)


def rwkv_heads(t):
    return t.reshape(t.shape[:-1] + (RWKV_HEADS, RWKV_HEAD))


def rwkv_features(p, shift, w0, w_up, a0, a_up, g_up, vecs):
    p = token_shift(p.astype(f32), shift[0], shift[1])
    r, k, v, wd, ad, gd = jnp.split(p, RWKV_SPLITS, axis=-1)
    k_k, k_a, r_k = vecs[0], vecs[1], vecs[2]
    kk = rwkv_heads(k * k_k)
    kk = kk * lax.rsqrt(jnp.sum(kk * kk, axis=-1, keepdims=True) + 1e-12)
    per_dir = []
    for d in range(2):
        w = jnp.exp(-RWKV_DECAY_SCALE * jax.nn.sigmoid(w0[d] + jnp.tanh(wd) @ w_up[d]))
        a = jax.nn.sigmoid(a0[d] + ad @ a_up[d])
        k_d = k * (1.0 + (a - 1.0) * k_a)
        per_dir.append((rwkv_heads(w), rwkv_heads(k_d), -kk, kk * rwkv_heads(a)))
    gate = jax.nn.sigmoid(gd) @ g_up
    bonus = jnp.sum(rwkv_heads(r * k * r_k), axis=-1, keepdims=True) * rwkv_heads(v)
    return rwkv_heads(r), rwkv_heads(v), per_dir, gate, bonus


def rwkv7_scan(r, v, w, k, a, b, s0, reverse):
    def step(s, inp):
        r_t, v_t, w_t, k_t, a_t, b_t = inp
        sa = jnp.einsum('bhvk,bhk->bhv', s, a_t)
        s = s * w_t[:, :, None, :] + sa[..., None] * b_t[:, :, None, :] + v_t[..., None] * k_t[:, :, None, :]
        return s, jnp.einsum('bhvk,bhk->bhv', s, r_t)
    xs = tuple(t.transpose(1, 0, 2, 3) for t in (r, v, w, k, a, b))
    s_fin, ys = lax.scan(step, s0, xs, reverse=reverse)
    return s_fin, ys.transpose(1, 0, 2, 3)


def rwkv_mixer(p_lat, p_ctx, shift, w0, w_up, a0, a_up, g_up, vecs):
    r_l, v_l, dirs_l, g_l, bonus_l = rwkv_features(p_lat, shift, w0, w_up, a0, a_up, g_up, vecs)
    r_c, v_c, dirs_c, g_c, bonus_c = rwkv_features(p_ctx, shift, w0, w_up, a0, a_up, g_up, vecs)
    s0 = jnp.zeros((p_lat.shape[0], RWKV_HEADS, RWKV_HEAD, RWKV_HEAD), f32)
    ys_l, ys_c = [], []
    for d, rev in enumerate((False, True)):
        s_ctx, y_c = rwkv7_scan(r_c, v_c, *dirs_c[d], s0, rev)
        _, y_l = rwkv7_scan(r_l, v_l, *dirs_l[d], s_ctx, rev)
        ys_c.append(y_c)
        ys_l.append(y_l)
    ln_g = rwkv_heads(vecs[3].astype(f32))
    ln_b = rwkv_heads(vecs[4].astype(f32))

    def finish(y, bonus, gate):
        y = head_norm(y, RWKV_GN_EPS) * ln_g + ln_b + bonus
        return y.reshape(y.shape[:-2] + (GROUP_W,)) * gate

    y_lat = finish(ys_l[0] + ys_l[1], bonus_l, g_l)
    y_ctx = finish(ys_c[0] + ys_c[1], bonus_c, g_c)
    return y_lat.astype(p_lat.dtype), y_ctx.astype(p_ctx.dtype)


def mla_project(p, q_norm, kv_norm, w_uq, w_ukv, angs):
    bsz, seq_len = p.shape[0], p.shape[1]
    c_q, c_kv, k_r = jnp.split(p, [MLA_Q_RANK, MLA_Q_RANK + MLA_KV_RANK], axis=-1)
    q = (rms_norm(c_q, q_norm) @ w_uq).reshape(bsz, seq_len, MLA_HEADS, MLA_NOPE + MLA_ROPE)
    kv = (rms_norm(c_kv, kv_norm) @ w_ukv).reshape(bsz, seq_len, MLA_HEADS, MLA_NOPE + MLA_V)
    q_nope, q_rope = q[..., :MLA_NOPE], q[..., MLA_NOPE:]
    k_nope, v = kv[..., :MLA_NOPE], kv[..., MLA_NOPE:]
    k_r = k_r[:, :, None, :]
    if angs is not None:
        q_rope = axial_rope(q_rope, angs)
        k_r = axial_rope(k_r, angs)
    k = jnp.concatenate([k_nope, jnp.broadcast_to(k_r, k_nope.shape[:-1] + (MLA_ROPE,))], axis=-1)
    q = jnp.concatenate([q_nope, q_rope], axis=-1)
    return q, k, v


def softmax_attend(q, k, v):
    s = jnp.einsum('bqhd,bkhd->bhqk', q, k).astype(f32) * MLA_SCALE
    pr = jax.nn.softmax(s, axis=-1).astype(v.dtype)
    return jnp.einsum('bhqk,bkhd->bqhd', pr, v)


def blocked_attend(q, k, v):
    bsz, seq_len, n_h, d_k = q.shape
    qb = q.reshape(bsz, seq_len // Q_BLOCK, Q_BLOCK, n_h, d_k).swapaxes(0, 1)
    out = lax.map(lambda qq: softmax_attend(qq, k, v), qb)
    return out.swapaxes(0, 1).reshape(bsz, seq_len, n_h, v.shape[-1])


def mla_mixer(p_lat, p_ctx, q_norm, kv_norm, w_uq, w_ukv, angs):
    q_l, k_l, v_l = mla_project(p_lat, q_norm, kv_norm, w_uq, w_ukv, angs)
    q_c, k_c, v_c = mla_project(p_ctx, q_norm, kv_norm, w_uq, w_ukv, None)
    y_c = softmax_attend(q_c, k_c, v_c)
    k_all = jnp.concatenate([k_l, k_c], axis=1)
    v_all = jnp.concatenate([v_l, v_c], axis=1)
    y_l = blocked_attend(q_l, k_all, v_all)
    return y_l.reshape(y_l.shape[:2] + (GROUP_W,)), y_c.reshape(y_c.shape[:2] + (GROUP_W,))


def conv_mixer(p, conv_w):
    b_gate, c_gate, u = jnp.split(p, 3, axis=-1)
    return b_gate * dwconv3(c_gate * u, conv_w)


def ret_project(p, angs):
    q, k, v, g = jnp.split(p.astype(f32), 4, axis=-1)
    hs = lambda t: t.reshape(t.shape[:-1] + (RET_HEADS, RET_HEAD))
    q, k, v = hs(q), hs(k), hs(v)
    if angs is not None:
        q = axial_rope(q, angs)
        k = axial_rope(k, angs)
    return q, k * RET_HEAD ** -0.5, v, g


def retention_chunked(q, k, v, log_gamma, s0, inclusive):
    bsz, seq_len, n_h, _ = q.shape
    d_v = v.shape[-1]
    n_chunks = seq_len // RET_CHUNK
    pos = jnp.arange(RET_CHUNK, dtype=f32)
    diff = pos[:, None] - pos[None, :]
    mask = (diff >= 0) if inclusive else (diff > 0)
    decay_in = jnp.where(mask[None], jnp.exp(jnp.where(mask, diff, 0.0)[None] * log_gamma[:, None, None]), 0.0)
    xi = jnp.exp((pos + 1.0)[:, None] * log_gamma[None, :])
    zeta = jnp.exp((RET_CHUNK - 1.0 - pos)[:, None] * log_gamma[None, :])
    g_chunk = jnp.exp(RET_CHUNK * log_gamma)

    def to_chunks(t):
        return t.reshape(bsz, n_chunks, RET_CHUNK, n_h, t.shape[-1]).swapaxes(0, 1)

    def step(s, blk):
        qb, kb, vb = blk
        att = jnp.einsum('bihd,bjhd->bhij', qb, kb) * decay_in
        inner = jnp.einsum('bhij,bjhe->bihe', att, vb)
        cross = jnp.einsum('bihd,bhde->bihe', qb, s) * xi[None, :, :, None]
        s_new = s * g_chunk[None, :, None, None] + jnp.einsum('bjhd,bjhe->bhde', kb * zeta[None, :, :, None], vb)
        return s_new, inner + cross

    s_fin, out = lax.scan(step, s0, (to_chunks(q), to_chunks(k), to_chunks(v)))
    return s_fin, out.swapaxes(0, 1).reshape(bsz, seq_len, n_h, d_v)


def retention_mixer(p_lat, p_ctx, ret_decay, gn_g, angs):
    log_gamma = -jnp.exp(ret_decay.astype(f32))
    q_l, k_l, v_l, g_l = ret_project(p_lat, angs)
    q_c, k_c, v_c, g_c = ret_project(p_ctx, None)
    s0 = jnp.zeros((p_lat.shape[0], RET_HEADS, RET_HEAD, RET_HEAD), f32)
    flip = lambda t: jnp.flip(t, axis=1)
    s_cf, o_cf = retention_chunked(q_c, k_c, v_c, log_gamma[0], s0, True)
    _, o_lf = retention_chunked(q_l, k_l, v_l, log_gamma[0], s_cf, True)
    s_cb, o_cb = retention_chunked(flip(q_c), flip(k_c), flip(v_c), log_gamma[1], s0, False)
    _, o_lb = retention_chunked(flip(q_l), flip(k_l), flip(v_l), log_gamma[1], s_cb, False)
    gn = gn_g.astype(f32).reshape(RET_HEADS, RET_HEAD)

    def finish(o, g):
        o = head_norm(o, GN_EPS) * gn
        return jax.nn.silu(g) * o.reshape(o.shape[:-2] + (GROUP_W,))

    y_lat = finish(o_lf + flip(o_lb), g_l)
    y_ctx = finish(o_cf + flip(o_cb), g_c)
    return y_lat.astype(p_lat.dtype), y_ctx.astype(p_ctx.dtype)


def conv_ffn(h, w_up, w_conv, w_down):
    u = dwconv3(h @ w_up, w_conv)
    gate, val = jnp.split(u, 2, axis=-1)
    return (jax.nn.silu(gate) * val) @ w_down


def layer_forward(x, xc, c, c_ctx, mod_w, mod_b, norm_g, w_in, rwkv_shift, rwkv_w0, rwkv_w_up,
                  rwkv_a0, rwkv_a_up, rwkv_g_up, rwkv_vecs, mla_q_norm, mla_kv_norm, mla_w_uq,
                  mla_w_ukv, conv_w, ret_decay, ret_gn_g, w_out, mlp_w_up, mlp_conv, mlp_w_down,
                  angs_mla, angs_ret, update_ctx):
    bsz = x.shape[0]
    m = (jax.nn.silu(c) @ mod_w + mod_b).reshape(bsz, N_MOD, 1, D_MODEL)
    mc = (jax.nn.silu(c_ctx) @ mod_w + mod_b).reshape(N_MOD, D_MODEL)

    h = rms_norm(x, norm_g[0]) * (1.0 + m[:, 1]) + m[:, 0]
    hc = rms_norm(xc, norm_g[0]) * (1.0 + mc[1]) + mc[0]
    pa, pb, pcv, pd = jnp.split(h @ w_in, GROUP_SPLITS, axis=-1)
    pa_c, pb_c, pcv_c, pd_c = jnp.split(hc @ w_in, GROUP_SPLITS, axis=-1)
    ya, ya_c = rwkv_mixer(pa, pa_c, rwkv_shift, rwkv_w0, rwkv_w_up, rwkv_a0, rwkv_a_up, rwkv_g_up, rwkv_vecs)
    yb, yb_c = mla_mixer(pb, pb_c, mla_q_norm, mla_kv_norm, mla_w_uq, mla_w_ukv, angs_mla)
    yd, yd_c = retention_mixer(pd, pd_c, ret_decay, ret_gn_g, angs_ret)
    y = jnp.concatenate([ya, yb, conv_mixer(pcv, conv_w), yd], axis=-1) @ w_out
    x = x + m[:, 2] * rms_norm(y, norm_g[1])
    if update_ctx:
        yc = jnp.concatenate([ya_c, yb_c, conv_mixer(pcv_c, conv_w), yd_c], axis=-1) @ w_out
        xc = xc + mc[2] * rms_norm(yc, norm_g[1])

    h = rms_norm(x, norm_g[2]) * (1.0 + m[:, 4]) + m[:, 3]
    x = x + m[:, 5] * rms_norm(conv_ffn(h, mlp_w_up, mlp_conv, mlp_w_down), norm_g[3])
    if update_ctx:
        hc = rms_norm(xc, norm_g[2]) * (1.0 + mc[4]) + mc[3]
        xc = xc + mc[5] * rms_norm(conv_ffn(hc, mlp_w_up, mlp_conv, mlp_w_down), norm_g[3])
    return x, xc


def setup_inputs(seed: int = 0) -> dict:
    key = jax.random.key(seed)
    ks = jax.random.split(key, 26)

    def nrm(k, shape, scale):
        return jax.random.normal(k, shape, f32) * scale

    nl = DEPTH
    h_idx = jnp.arange(RET_HEADS, dtype=f32)
    ret_base = jnp.log(-jnp.log1p(-jnp.exp2(-5.0 - h_idx)))
    vec_off = jnp.array([0.85, 1.0, 0.0, 1.0, 0.0], f32)[None, :, None]
    vec_scale = jnp.array([0.05, 0.05, 0.1, 0.02, 0.02], f32)[None, :, None]
    return {
        'x': nrm(ks[0], (BATCH, SEQ, D_MODEL), 1.0),
        'c': nrm(ks[1], (BATCH, D_MODEL), 1.0),
        'ctx': nrm(ks[2], (BATCH, CTX_LEN, D_MODEL), 1.0),
        'c_ctx': nrm(ks[3], (D_MODEL,), 1.0),
        'mod_w': nrm(ks[4], (nl, D_MODEL, N_MOD * D_MODEL), 0.5 * D_MODEL ** -0.5),
        'mod_b': nrm(ks[5], (nl, N_MOD * D_MODEL), 0.02),
        'norm_g': 1.0 + nrm(ks[6], (nl, 4, D_MODEL), 0.02),
        'w_in': nrm(ks[7], (nl, D_MODEL, IN_W), D_MODEL ** -0.5),
        'rwkv_shift': 0.25 + nrm(ks[8], (nl, 2, RWKV_IN), 0.05),
        'rwkv_w0': -3.0 + nrm(ks[9], (nl, 2, GROUP_W), 1.0),
        'rwkv_w_up': nrm(ks[10], (nl, 2, RWKV_DECAY_LORA, GROUP_W), 0.1 * RWKV_DECAY_LORA ** -0.5),
        'rwkv_a0': nrm(ks[11], (nl, 2, GROUP_W), 0.5),
        'rwkv_a_up': nrm(ks[12], (nl, 2, RWKV_ICLR_LORA, GROUP_W), 0.1 * RWKV_ICLR_LORA ** -0.5),
        'rwkv_g_up': nrm(ks[13], (nl, RWKV_GATE_LORA, GROUP_W), RWKV_GATE_LORA ** -0.5),
        'rwkv_vecs': vec_off + vec_scale * jax.random.normal(ks[14], (nl, 5, GROUP_W), f32),
        'mla_q_norm': 1.0 + nrm(ks[15], (nl, MLA_Q_RANK), 0.02),
        'mla_kv_norm': 1.0 + nrm(ks[16], (nl, MLA_KV_RANK), 0.02),
        'mla_w_uq': nrm(ks[17], (nl, MLA_Q_RANK, MLA_HEADS * (MLA_NOPE + MLA_ROPE)), MLA_Q_RANK ** -0.5),
        'mla_w_ukv': nrm(ks[18], (nl, MLA_KV_RANK, MLA_HEADS * (MLA_NOPE + MLA_V)), MLA_KV_RANK ** -0.5),
        'conv_w': nrm(ks[19], (nl, CONV_WIDTH, GROUP_W), CONV_WIDTH ** -0.5),
        'ret_decay': ret_base[None, None, :] + nrm(ks[20], (nl, 2, RET_HEADS), 0.1),
        'ret_gn_g': 1.0 + nrm(ks[21], (nl, GROUP_W), 0.02),
        'w_out': nrm(ks[22], (nl, MIX_W, D_MODEL), MIX_W ** -0.5),
        'mlp_w_up': nrm(ks[23], (nl, D_MODEL, 2 * D_FF), D_MODEL ** -0.5),
        'mlp_conv': nrm(ks[24], (nl, CONV_WIDTH, 2 * D_FF), CONV_WIDTH ** -0.5),
        'mlp_w_down': nrm(ks[25], (nl, D_FF, D_MODEL), D_FF ** -0.5),
    }


def reference(x, c, ctx, c_ctx, mod_w, mod_b, norm_g, w_in, rwkv_shift, rwkv_w0, rwkv_w_up,
              rwkv_a0, rwkv_a_up, rwkv_g_up, rwkv_vecs, mla_q_norm, mla_kv_norm, mla_w_uq,
              mla_w_ukv, conv_w, ret_decay, ret_gn_g, w_out, mlp_w_up, mlp_conv, mlp_w_down):
    rows = x.shape[1] // GRID_W
    angs_mla = axial_rope_angles(rows, MLA_ROPE)
    angs_ret = axial_rope_angles(rows, RET_HEAD)
    xc = ctx
    for l in range(DEPTH):
        x, xc = layer_forward(
            x, xc, c, c_ctx, mod_w[l], mod_b[l], norm_g[l], w_in[l], rwkv_shift[l], rwkv_w0[l],
            rwkv_w_up[l], rwkv_a0[l], rwkv_a_up[l], rwkv_g_up[l], rwkv_vecs[l], mla_q_norm[l],
            mla_kv_norm[l], mla_w_uq[l], mla_w_ukv[l], conv_w[l], ret_decay[l], ret_gn_g[l],
            w_out[l], mlp_w_up[l], mlp_conv[l], mlp_w_down[l], angs_mla, angs_ret,
            update_ctx=(l < DEPTH - 1))
    return x
```

```python
import functools
import math

import jax
import jax.numpy as jnp
from jax import lax
from jax.experimental import pallas as pl
from jax.experimental.pallas import tpu as pltpu

F32 = jnp.float32
BF16 = jnp.bfloat16

D_MODEL = 2048
GRID_W = 64
GROUP_W = 512
N_MOD = 6
NORM_EPS = 1e-6
ROPE_BASE = 10000.0

RWKV_HEAD = 64
RWKV_HEADS = GROUP_W // RWKV_HEAD
RWKV_DECAY_LORA = 96
RWKV_ICLR_LORA = 96
RWKV_GATE_LORA = 256
RWKV_IN = 3 * GROUP_W + RWKV_DECAY_LORA + RWKV_ICLR_LORA + RWKV_GATE_LORA
RWKV_LORA_PAD = 128
RWKV_IN_PAD = 3 * GROUP_W + 2 * RWKV_LORA_PAD + RWKV_GATE_LORA
RWKV_DECAY_SCALE = math.exp(-0.5)
RWKV_GN_EPS = 64e-5
RWKV_CHUNK = 64

MLA_HEADS = 4
MLA_NOPE = 128
MLA_ROPE = 64
MLA_V = 128
MLA_Q_RANK = 384
MLA_KV_RANK = 256
MLA_IN = MLA_Q_RANK + MLA_KV_RANK + MLA_ROPE
MLA_IN_PAD = 768
MLA_HEAD_PAD = 256
MLA_SCALE = (MLA_NOPE + MLA_ROPE) ** -0.5

CONV_IN = 3 * GROUP_W

RET_HEADS = 4
RET_HEAD = 128
RET_IN = 4 * GROUP_W
GN_EPS = 1e-5

D_FF = 5632
VMEM_LIMIT = 56 * 1024 * 1024


def _cparams(*sem):
    return pltpu.CompilerParams(dimension_semantics=sem, vmem_limit_bytes=VMEM_LIMIT)


def _pick_tile(n, target, mult=16):
    best = None
    for t in range(mult, min(n, target) + 1, mult):
        if n % t == 0:
            best = t
    assert best is not None, (n, target)
    return best


def _dot(a, b):
    return jnp.dot(a.astype(BF16), b.astype(BF16), preferred_element_type=F32)


def _dot_nt(a, b):
    return lax.dot_general(a.astype(BF16), b.astype(BF16), (((1,), (1,)), ((), ())),
                           preferred_element_type=F32)


def _dot_tn(a, b):
    return lax.dot_general(a.astype(BF16), b.astype(BF16), (((0,), (0,)), ((), ())),
                           preferred_element_type=F32)


def _split3(x):
    hi = x.astype(BF16)
    r1 = x - hi.astype(F32)
    mid = r1.astype(BF16)
    lo = (r1 - mid.astype(F32)).astype(BF16)
    return hi, mid, lo


def _dot_wide_rhs(m, x):
    hi, mid, lo = _split3(x)
    return (jnp.dot(m, hi, preferred_element_type=F32) + jnp.dot(m, mid, preferred_element_type=F32)
            + jnp.dot(m, lo, preferred_element_type=F32))


def _dot_wide_lhs(x, m):
    hi, mid, lo = _split3(x)
    return (jnp.dot(hi, m, preferred_element_type=F32) + jnp.dot(mid, m, preferred_element_type=F32)
            + jnp.dot(lo, m, preferred_element_type=F32))


def _dot_hl(p, x):
    m, n = p.shape[0], x.shape[1]
    p_hi = p.astype(BF16)
    p_lo = (p - p_hi.astype(F32)).astype(BF16)
    x_hi = x.astype(BF16)
    x_lo = (x - x_hi.astype(F32)).astype(BF16)
    o = jnp.dot(jnp.concatenate([p_hi, p_lo], axis=0), jnp.concatenate([x_hi, x_lo], axis=1),
                preferred_element_type=F32)
    return o[:m, :n] + o[:m, n:] + o[m:, :n]


def _sigmoid(x):
    return 1.0 / (1.0 + jnp.exp(-x))


def _silu(x):
    return x * _sigmoid(x)


def _rms(x, g):
    ms = jnp.mean(x * x, axis=-1, keepdims=True)
    return x * lax.rsqrt(ms + NORM_EPS) * g


def _ctx_rows(tile_idx, tm, ctx_len):
    rows = tile_idx * tm + lax.broadcasted_iota(jnp.int32, (tm, 1), 0)
    return rows < ctx_len


def _mod_row(is_ctx, mc_ref, mb_ref, k):
    return jnp.where(is_ctx, mc_ref[k:k + 1, :], mb_ref[0, k:k + 1, :])


def _shift_rows(u, first_row, last_row):
    n = u.shape[0]
    rows = lax.broadcasted_iota(jnp.int32, (n, 1), 0)
    prev = jnp.where(rows == 0, first_row, pltpu.roll(u, 1, 0))
    nxt = jnp.where(rows == n - 1, last_row, pltpu.roll(u, n - 1, 0))
    return prev, nxt


def _rope(x, cos, sa, sb, half):
    n = x.shape[-1]
    return x * cos + pltpu.roll(x, n - half, 1) * sa + pltpu.roll(x, half, 1) * sb


def _mod_kernel(c_ref, w_ref, b_ref, o_ref):
    o_ref[0] = _dot(_silu(c_ref[...]), w_ref[0]) + b_ref[0]


def _modulation(c_pad, mod_w, mod_b):
    depth, d, n = mod_w.shape
    tn = 1024
    return pl.pallas_call(
        _mod_kernel,
        grid=(depth, n // tn),
        in_specs=[pl.BlockSpec((8, d), lambda l, j: (0, 0)),
                  pl.BlockSpec((1, d, tn), lambda l, j: (l, 0, j)),
                  pl.BlockSpec((1, 1, tn), lambda l, j: (l, 0, j))],
        out_specs=pl.BlockSpec((1, 8, tn), lambda l, j: (l, 0, j)),
        out_shape=jax.ShapeDtypeStruct((depth, 8, n), F32),
        compiler_params=_cparams("arbitrary", "arbitrary"),
    )(c_pad, mod_w, mod_b.reshape(depth, 1, n))


def _prologue_kernel(x_ref, mc_ref, mb_ref, g_ref, h_ref, *, tm, ctx_len):
    is_ctx = _ctx_rows(pl.program_id(1), tm, ctx_len)
    h = _rms(x_ref[...], g_ref[...]) * (1.0 + _mod_row(is_ctx, mc_ref, mb_ref, 1)) + _mod_row(is_ctx, mc_ref, mb_ref, 0)
    h_ref[...] = h.astype(BF16)


def _prologue(x, mc, mb, g, *, nb, tb, ctx_len):
    d = x.shape[1]
    tm = _pick_tile(tb, 768)
    nt = tb // tm
    return pl.pallas_call(
        functools.partial(_prologue_kernel, tm=tm, ctx_len=ctx_len),
        grid=(nb, nt),
        in_specs=[pl.BlockSpec((tm, d), lambda b, t: (b * nt + t, 0)),
                  pl.BlockSpec((N_MOD, d), lambda b, t: (0, 0)),
                  pl.BlockSpec((1, N_MOD, d), lambda b, t: (b, 0, 0)),
                  pl.BlockSpec((1, d), lambda b, t: (0, 0))],
        out_specs=pl.BlockSpec((tm, d), lambda b, t: (b * nt + t, 0)),
        out_shape=jax.ShapeDtypeStruct(x.shape, BF16),
        compiler_params=_cparams("arbitrary", "arbitrary"),
    )(x, mc, mb, g)


def _matmul_kernel(a_ref, w_ref, o_ref):
    o_ref[...] = jnp.dot(a_ref[...], w_ref[...], preferred_element_type=F32).astype(o_ref.dtype)


def _matmul(a, w, out_dtype, *, nb, tb, tn):
    k, n = w.shape
    assert n % tn == 0
    return pl.pallas_call(
        _matmul_kernel,
        grid=(nb, n // tn),
        in_specs=[pl.BlockSpec((tb, k), lambda b, j: (b, 0)),
                  pl.BlockSpec((k, tn), lambda b, j: (0, j))],
        out_specs=pl.BlockSpec((tb, tn), lambda b, j: (b, j)),
        out_shape=jax.ShapeDtypeStruct((a.shape[0], n), out_dtype),
        compiler_params=_cparams("arbitrary", "arbitrary"),
    )(a, w)


def _rwkv_feat_kernel(p_ref, pprev_ref, pnext_ref, shift_ref, w0_ref, wup_ref, a0_ref, aup_ref, gup_ref,
                      vecs_ref, ones_ref, tri_ref,
                      at_ref, rt_ref, bt_ref, kt_ref, wc_ref, v_ref, gate_ref, bonus_ref,
                      *, tm, ctx_tiles, nt):
    t = pl.program_id(1)
    first = jnp.logical_or(t == 0, t == ctx_tiles)
    last = jnp.logical_or(t == ctx_tiles - 1, t == nt - 1)
    u = p_ref[...]
    halo_prev = jnp.where(first, 0.0, pprev_ref[7:8, :])
    halo_next = jnp.where(last, 0.0, pnext_ref[0:1, :])
    prev, nxt = _shift_rows(u, halo_prev, halo_next)
    p = u + shift_ref[0:1, :] * (prev - u) + shift_ref[1:2, :] * (nxt - u)

    gw = GROUP_W
    r = p[:, 0:gw]
    k = p[:, gw:2 * gw]
    v = p[:, 2 * gw:3 * gw]
    wd = p[:, 3 * gw:3 * gw + RWKV_LORA_PAD]
    ad = p[:, 3 * gw + RWKV_LORA_PAD:3 * gw + 2 * RWKV_LORA_PAD]
    gd = p[:, 3 * gw + 2 * RWKV_LORA_PAD:]
    k_k = vecs_ref[0:1, :]
    k_a = vecs_ref[1:2, :]
    r_k = vecs_ref[2:3, :]
    ones = ones_ref[...]

    kk = k * k_k
    kk = kk * lax.rsqrt(_dot_wide_lhs(kk * kk, ones) + 1e-12)
    v_ref[...] = v.astype(BF16)
    gate_ref[...] = _dot(_sigmoid(gd), gup_ref[...])
    bonus_ref[...] = _dot_wide_lhs(r * k * r_k, ones) * v

    tanh_wd = jnp.tanh(wd)
    nch = tm // RWKV_CHUNK
    for d in range(2):
        lw = -RWKV_DECAY_SCALE * _sigmoid(w0_ref[d:d + 1, :] + _dot(tanh_wd, wup_ref[d]))
        l_inc = _dot_wide_rhs(tri_ref[d], lw)
        l_exc = l_inc - lw
        asig = _sigmoid(a0_ref[d:d + 1, :] + _dot(ad, aup_ref[d]))
        k_d = k * (1.0 + (asig - 1.0) * k_a)
        e_inc = jnp.exp(l_inc)
        e_neg = jnp.exp(-l_inc)
        at_ref[d] = (-kk * jnp.exp(l_exc)).astype(BF16)
        rt_ref[d] = (r * e_inc).astype(BF16)
        bt_ref[d] = (kk * asig * e_neg).astype(BF16)
        kt_ref[d] = (k_d * e_neg).astype(BF16)
        end = RWKV_CHUNK - 1 if d == 0 else 0
        rows = [e_inc[c * RWKV_CHUNK + end:c * RWKV_CHUNK + end + 1, :] for c in range(nch)]
        rows.append(jnp.zeros((8 - nch, gw), F32))
        wc_ref[d] = jnp.concatenate(rows, axis=0)


def _rwkv_features(pa, shift, w0, wup, a0, aup, gup, vecs, ones_bd, tri, *, nb, tb, ctx_len):
    r_tot = pa.shape[0]
    tm = 256
    assert ctx_len % tm == 0 and tb % tm == 0
    nt = tb // tm
    hb = tm // 8
    nblk8 = r_tot // 8
    gw = GROUP_W
    row = lambda b, t: (b * nt + t, 0)
    drow = lambda b, t: (0, b * nt + t, 0)
    const2 = lambda b, t: (0, 0)
    const3 = lambda b, t: (0, 0, 0)
    feat = jax.ShapeDtypeStruct((2, r_tot, gw), BF16)
    return pl.pallas_call(
        functools.partial(_rwkv_feat_kernel, tm=tm, ctx_tiles=ctx_len // tm, nt=nt),
        grid=(nb, nt),
        in_specs=[pl.BlockSpec((tm, RWKV_IN_PAD), row),
                  pl.BlockSpec((8, RWKV_IN_PAD), lambda b, t: (jnp.maximum((b * nt + t) * hb - 1, 0), 0)),
                  pl.BlockSpec((8, RWKV_IN_PAD), lambda b, t: (jnp.minimum((b * nt + t + 1) * hb, nblk8 - 1), 0)),
                  pl.BlockSpec((2, RWKV_IN_PAD), const2),
                  pl.BlockSpec((2, gw), const2),
                  pl.BlockSpec((2, RWKV_LORA_PAD, gw), const3),
                  pl.BlockSpec((2, gw), const2),
                  pl.BlockSpec((2, RWKV_LORA_PAD, gw), const3),
                  pl.BlockSpec((RWKV_GATE_LORA, gw), const2),
                  pl.BlockSpec((5, gw), const2),
                  pl.BlockSpec((gw, gw), const2),
                  pl.BlockSpec((2, tm, tm), const3)],
        out_specs=[pl.BlockSpec((2, tm, gw), drow)] * 4
        + [pl.BlockSpec((2, 8, gw), drow),
           pl.BlockSpec((tm, gw), row), pl.BlockSpec((tm, gw), row), pl.BlockSpec((tm, gw), row)],
        out_shape=[feat, feat, feat, feat,
                   jax.ShapeDtypeStruct((2, r_tot // tm * 8, gw), F32),
                   jax.ShapeDtypeStruct((r_tot, gw), BF16),
                   jax.ShapeDtypeStruct((r_tot, gw), F32),
                   jax.ShapeDtypeStruct((r_tot, gw), F32)],
        compiler_params=_cparams("arbitrary", "arbitrary"),
    )(pa, pa, pa, shift, w0, wup, a0, aup, gup, vecs, ones_bd, tri)


def _rwkv_scan_kernel(at_ref, rt_ref, bt_ref, kt_ref, wc_ref, v_ref, y_ref, s_ref, *, n_chunks, ctx_chunks, feat_tm):
    d = pl.program_id(1)
    c_len = RWKV_CHUNK
    hd = RWKV_HEAD
    s_ref[...] = jnp.zeros_like(s_ref)
    ti = lax.broadcasted_iota(jnp.int32, (c_len, c_len), 0)
    tj = lax.broadcasted_iota(jnp.int32, (c_len, c_len), 1)
    diff = (ti - tj) * (1 - 2 * d)
    strict = diff > 0
    incl = diff >= 0
    per_tile = feat_tm // c_len

    def body(i, carry):
        c_bwd = jnp.where(i < ctx_chunks, ctx_chunks - 1 - i, n_chunks - 1 + ctx_chunks - i)
        c = jnp.where(d == 0, i, c_bwd)
        row0 = pl.multiple_of(c * c_len, c_len)
        a_all = at_ref[0, pl.ds(row0, c_len), :]
        r_all = rt_ref[0, pl.ds(row0, c_len), :]
        b_all = bt_ref[0, pl.ds(row0, c_len), :]
        k_all = kt_ref[0, pl.ds(row0, c_len), :]
        v_all = v_ref[pl.ds(row0, c_len), :]
        wc = wc_ref[0, pl.ds((c // per_tile) * 8 + c % per_tile, 1), :]
        for h in range(RWKV_HEADS):
            sl = slice(h * hd, (h + 1) * hd)
            a_, r_, b_, k_, v_ = a_all[:, sl], r_all[:, sl], b_all[:, sl], k_all[:, sl], v_all[:, sl]
            ar = jnp.concatenate([a_, r_], axis=0)
            bk = jnp.concatenate([b_, k_], axis=0)
            aa = _dot_nt(ar, bk)
            a_ab = jnp.where(strict, aa[:c_len, :c_len], 0.0)
            a_ak = jnp.where(strict, aa[:c_len, c_len:], 0.0)
            a_rb = jnp.where(incl, aa[c_len:, :c_len], 0.0)
            a_rk = jnp.where(incl, aa[c_len:, c_len:], 0.0)
            s = s_ref[h]
            arz = _dot_nt(ar, s)
            x = arz[:c_len] + _dot(a_ak, v_)
            p = a_ab
            for lvl in range(6):
                x = x + _dot_hl(p, x)
                if lvl < 5:
                    p = _dot_hl(p, p)
            y = arz[c_len:] + _dot(a_rb, x) + _dot(a_rk, v_)
            y_ref[0, pl.ds(row0, c_len), sl] = y
            s_ref[h] = (s + _dot_tn(x, b_) + _dot_tn(v_, k_)) * wc[:, sl]
        return carry

    lax.fori_loop(0, n_chunks, body, 0)


def _rwkv_scan(at, rt, bt, kt, wc, v, *, nb, tb, ctx_len, feat_tm):
    r_tot = v.shape[0]
    gw = GROUP_W
    n_chunks = tb // RWKV_CHUNK
    wrows = tb // feat_tm * 8
    dspec = pl.BlockSpec((1, tb, gw), lambda b, d: (d, b, 0))
    return pl.pallas_call(
        functools.partial(_rwkv_scan_kernel, n_chunks=n_chunks, ctx_chunks=ctx_len // RWKV_CHUNK, feat_tm=feat_tm),
        grid=(nb, 2),
        in_specs=[dspec, dspec, dspec, dspec,
                  pl.BlockSpec((1, wrows, gw), lambda b, d: (d, b, 0)),
                  pl.BlockSpec((tb, gw), lambda b, d: (b, 0))],
        out_specs=pl.BlockSpec((1, tb, gw), lambda b, d: (d, b, 0)),
        out_shape=jax.ShapeDtypeStruct((2, r_tot, gw), F32),
        scratch_shapes=[pltpu.VMEM((RWKV_HEADS, RWKV_HEAD, RWKV_HEAD), F32)],
        compiler_params=_cparams("arbitrary", "arbitrary"),
    )(at, rt, bt, kt, wc, v)


def _rwkv_finish_kernel(y_ref, gate_ref, bonus_ref, vecs_ref, ones_ref, o_ref):
    ones = ones_ref[...]
    y = y_ref[0] + y_ref[1]
    inv = 1.0 / RWKV_HEAD
    mu = _dot_wide_lhs(y, ones) * inv
    yc = y - mu
    var = _dot_wide_lhs(yc * yc, ones) * inv
    out = yc * lax.rsqrt(var + RWKV_GN_EPS) * vecs_ref[3:4, :] + vecs_ref[4:5, :] + bonus_ref[...]
    o_ref[...] = (out * gate_ref[...]).astype(o_ref.dtype)


def _rwkv_finish(y, gate, bonus, vecs, ones_bd, *, nb, tb):
    r_tot = gate.shape[0]
    gw = GROUP_W
    tm = _pick_tile(tb, 768)
    nt = tb // tm
    row = lambda b, t: (b * nt + t, 0)
    return pl.pallas_call(
        _rwkv_finish_kernel,
        grid=(nb, nt),
        in_specs=[pl.BlockSpec((2, tm, gw), lambda b, t: (0, b * nt + t, 0)),
                  pl.BlockSpec((tm, gw), row), pl.BlockSpec((tm, gw), row),
                  pl.BlockSpec((5, gw), lambda b, t: (0, 0)),
                  pl.BlockSpec((gw, gw), lambda b, t: (0, 0))],
        out_specs=pl.BlockSpec((tm, gw), row),
        out_shape=jax.ShapeDtypeStruct((r_tot, gw), BF16),
        compiler_params=_cparams("arbitrary", "arbitrary"),
    )(y, gate, bonus, vecs, ones_bd)


def _mla_proj_kernel(p_ref, qn_ref, kvn_ref, wq_ref, wk_ref, wv_ref, cos_ref, sa_ref, sb_ref,
                     q_ref, k_ref, v_ref):
    p = p_ref[...].astype(F32)
    c_q = p[:, :MLA_Q_RANK]
    c_kv = p[:, MLA_Q_RANK:MLA_Q_RANK + MLA_KV_RANK]
    k_r = p[:, MLA_Q_RANK + MLA_KV_RANK:]
    n_q = _rms(c_q, qn_ref[...])
    n_kv = _rms(c_kv, kvn_ref[...])
    q = _dot(n_q, wq_ref[...])
    kx = _dot(n_kv, wk_ref[...])
    v_ref[...] = _dot(n_kv, wv_ref[...]).astype(BF16)
    cos, sa, sb = cos_ref[...], sa_ref[...], sb_ref[...]
    kr_blk = jnp.concatenate([jnp.zeros_like(k_r), k_r], axis=1)
    hp = MLA_HEAD_PAD
    for h in range(MLA_HEADS):
        sl = slice(h * hp, (h + 1) * hp)
        q_ref[:, sl] = _rope(q[:, sl], cos, sa, sb, MLA_ROPE // 4).astype(BF16)
        k_ref[:, sl] = _rope(kx[:, sl] + kr_blk, cos, sa, sb, MLA_ROPE // 4).astype(BF16)


def _mla_project(pb, qn, kvn, wq, wk, wv, cos, sa, sb, *, nb, tb):
    r_tot = pb.shape[0]
    tm = _pick_tile(tb, 768)
    nt = tb // tm
    row = lambda b, t: (b * nt + t, 0)
    trow = lambda b, t: (t, 0)
    c2 = lambda b, t: (0, 0)
    hw = MLA_HEADS * MLA_HEAD_PAD
    return pl.pallas_call(
        _mla_proj_kernel,
        grid=(nb, nt),
        in_specs=[pl.BlockSpec((tm, MLA_IN_PAD), row),
                  pl.BlockSpec((1, MLA_Q_RANK), c2), pl.BlockSpec((1, MLA_KV_RANK), c2),
                  pl.BlockSpec((MLA_Q_RANK, hw), c2), pl.BlockSpec((MLA_KV_RANK, hw), c2),
                  pl.BlockSpec((MLA_KV_RANK, GROUP_W), c2),
                  pl.BlockSpec((tm, MLA_HEAD_PAD), trow), pl.BlockSpec((tm, MLA_HEAD_PAD), trow),
                  pl.BlockSpec((tm, MLA_HEAD_PAD), trow)],
        out_specs=[pl.BlockSpec((tm, hw), row), pl.BlockSpec((tm, hw), row), pl.BlockSpec((tm, GROUP_W), row)],
        out_shape=[jax.ShapeDtypeStruct((r_tot, hw), BF16), jax.ShapeDtypeStruct((r_tot, hw), BF16),
                   jax.ShapeDtypeStruct((r_tot, GROUP_W), BF16)],
        compiler_params=_cparams("arbitrary", "arbitrary"),
    )(pb, qn, kvn, wq, wk, wv, cos, sa, sb)


def _softmax_pv(s, v):
    s = s * MLA_SCALE
    m = jnp.max(s, axis=-1, keepdims=True)
    e = jnp.exp(s - m)
    l = jnp.sum(e, axis=-1, keepdims=True)
    return _dot(e, v) / l


def _mla_attn_kernel(q_ref, k_ref, v_ref, o_ref, *, ctx_len, tq, n_qt):
    k_all = k_ref[...]
    v_all = v_ref[...]
    o_ref[0:ctx_len, :] = _softmax_pv(_dot_nt(q_ref[0:ctx_len, :], k_all[0:ctx_len]), v_all[0:ctx_len]).astype(BF16)

    def body(i, carry):
        row0 = pl.multiple_of(ctx_len + i * tq, tq)
        s = _dot_nt(q_ref[pl.ds(row0, tq), :], k_all)
        o_ref[pl.ds(row0, tq), :] = _softmax_pv(s, v_all).astype(BF16)
        return carry

    lax.fori_loop(0, n_qt, body, 0)


def _mla_attention(q, k, v, *, nb, tb, ctx_len):
    r_tot = q.shape[0]
    tq = 256
    assert (tb - ctx_len) % tq == 0 and ctx_len % tq == 0
    hp = MLA_HEAD_PAD
    return pl.pallas_call(
        functools.partial(_mla_attn_kernel, ctx_len=ctx_len, tq=tq, n_qt=(tb - ctx_len) // tq),
        grid=(nb, MLA_HEADS),
        in_specs=[pl.BlockSpec((tb, hp), lambda b, h: (b, h)),
                  pl.BlockSpec((tb, hp), lambda b, h: (b, h)),
                  pl.BlockSpec((tb, MLA_V), lambda b, h: (b, h))],
        out_specs=pl.BlockSpec((tb, MLA_V), lambda b, h: (b, h)),
        out_shape=jax.ShapeDtypeStruct((r_tot, GROUP_W), BF16),
        compiler_params=_cparams("arbitrary", "arbitrary"),
    )(q, k, v)


def _seq_edges(n, ctx_len):
    rows = lax.broadcasted_iota(jnp.int32, (n, 1), 0)
    starts = jnp.logical_or(rows == 0, rows == ctx_len)
    ends = jnp.logical_or(rows == ctx_len - 1, rows == n - 1)
    return starts, ends


def _dwconv3_rows(u, w_ref, starts, ends):
    n = u.shape[0]
    prev = jnp.where(starts, 0.0, pltpu.roll(u, 1, 0))
    nxt = jnp.where(ends, 0.0, pltpu.roll(u, n - 1, 0))
    return prev * w_ref[0:1, :] + u * w_ref[1:2, :] + nxt * w_ref[2:3, :]


def _conv_mix_kernel(b_ref, c_ref, u_ref, w_ref, o_ref, *, ctx_len):
    starts, ends = _seq_edges(b_ref.shape[0], ctx_len)
    z = c_ref[...].astype(F32) * u_ref[...].astype(F32)
    o_ref[...] = (b_ref[...].astype(F32) * _dwconv3_rows(z, w_ref, starts, ends)).astype(o_ref.dtype)


def _conv_mixer(pc, conv_w, *, nb, tb, ctx_len):
    r_tot = pc.shape[0]
    tc = 256
    nj = GROUP_W // tc
    return pl.pallas_call(
        functools.partial(_conv_mix_kernel, ctx_len=ctx_len),
        grid=(nb, nj),
        in_specs=[pl.BlockSpec((tb, tc), lambda b, j: (b, j)),
                  pl.BlockSpec((tb, tc), lambda b, j: (b, nj + j)),
                  pl.BlockSpec((tb, tc), lambda b, j: (b, 2 * nj + j)),
                  pl.BlockSpec((3, tc), lambda b, j: (0, j))],
        out_specs=pl.BlockSpec((tb, tc), lambda b, j: (b, j)),
        out_shape=jax.ShapeDtypeStruct((r_tot, GROUP_W), BF16),
        compiler_params=_cparams("arbitrary", "arbitrary"),
    )(pc, pc, pc, conv_w)


def _ret_kernel(q_ref, k_ref, v_ref, g_ref, dec_ref, gn_ref, cos_ref, sa_ref, sb_ref, o_ref, qs_ref, ks_ref,
                *, ctx_len, seq_len, tq):
    cos, sa, sb = cos_ref[...], sa_ref[...], sb_ref[...]
    qs_ref[...] = _rope(q_ref[...].astype(F32), cos, sa, sb, RET_HEAD // 4).astype(BF16)
    ks_ref[...] = _rope(k_ref[...].astype(F32), cos, sa, sb, RET_HEAD // 4).astype(BF16)
    lgf = -jnp.exp(dec_ref[0, 0:1, 0:1])
    lgb = -jnp.exp(dec_ref[0, 1:2, 0:1])
    gn = gn_ref[...]
    scale = RET_HEAD ** -0.5

    def finish(o, g):
        mu = jnp.mean(o, axis=-1, keepdims=True)
        oc = o - mu
        var = jnp.mean(oc * oc, axis=-1, keepdims=True)
        return _silu(g) * (oc * lax.rsqrt(var + GN_EPS) * gn)

    def bidir_decay(dist):
        fwd = dist >= 0
        ad = jnp.abs(dist).astype(F32)
        return jnp.exp(ad * jnp.where(fwd, lgf, lgb))

    k_c = ks_ref[0:ctx_len, :]
    v_c = v_ref[0:ctx_len, :]
    k_l = ks_ref[ctx_len:, :]
    v_l = v_ref[ctx_len:, :]

    ci = lax.broadcasted_iota(jnp.int32, (ctx_len, ctx_len), 0)
    cj = lax.broadcasted_iota(jnp.int32, (ctx_len, ctx_len), 1)
    s_c = _dot_nt(qs_ref[0:ctx_len, :], k_c) * bidir_decay(ci - cj)
    o_c = _dot(s_c, v_c) * scale
    o_ref[0:ctx_len, :] = finish(o_c, g_ref[0:ctx_len, :].astype(F32)).astype(o_ref.dtype)

    ri = lax.broadcasted_iota(jnp.int32, (tq, ctx_len), 0)
    rj = lax.broadcasted_iota(jnp.int32, (tq, ctx_len), 1)
    li = lax.broadcasted_iota(jnp.int32, (tq, seq_len), 0)
    lj = lax.broadcasted_iota(jnp.int32, (tq, seq_len), 1)

    def body(i, carry):
        row0 = pl.multiple_of(ctx_len + i * tq, tq)
        q = qs_ref[pl.ds(row0, tq), :]
        n_ctx = (i * tq + ri)
        d_ctx = (jnp.exp((n_ctx + ctx_len - rj).astype(F32) * lgf)
                 + jnp.exp((seq_len - n_ctx + rj).astype(F32) * lgb))
        s1 = _dot_nt(q, k_c) * d_ctx
        s2 = _dot_nt(q, k_l) * bidir_decay(i * tq + li - lj)
        o = (_dot(s1, v_c) + _dot(s2, v_l)) * scale
        o_ref[pl.ds(row0, tq), :] = finish(o, g_ref[pl.ds(row0, tq), :].astype(F32)).astype(o_ref.dtype)
        return carry

    lax.fori_loop(0, seq_len // tq, body, 0)


def _retention(pd, dec, gn, cos, sa, sb, *, nb, tb, ctx_len):
    r_tot = pd.shape[0]
    hd = RET_HEAD
    nh = RET_HEADS
    tq = 256
    seq_len = tb - ctx_len
    assert seq_len % tq == 0
    tbl = pl.BlockSpec((tb, hd), lambda b, h: (0, 0))
    return pl.pallas_call(
        functools.partial(_ret_kernel, ctx_len=ctx_len, seq_len=seq_len, tq=tq),
        grid=(nb, nh),
        in_specs=[pl.BlockSpec((tb, hd), lambda b, h: (b, h)),
                  pl.BlockSpec((tb, hd), lambda b, h: (b, nh + h)),
                  pl.BlockSpec((tb, hd), lambda b, h: (b, 2 * nh + h)),
                  pl.BlockSpec((tb, hd), lambda b, h: (b, 3 * nh + h)),
                  pl.BlockSpec((1, 8, 128), lambda b, h: (h, 0, 0)),
                  pl.BlockSpec((1, hd), lambda b, h: (0, h)),
                  tbl, tbl, tbl],
        out_specs=pl.BlockSpec((tb, hd), lambda b, h: (b, h)),
        out_shape=jax.ShapeDtypeStruct((r_tot, GROUP_W), BF16),
        scratch_shapes=[pltpu.VMEM((tb, hd), BF16), pltpu.VMEM((tb, hd), BF16)],
        compiler_params=_cparams("arbitrary", "arbitrary"),
    )(pd, pd, pd, pd, dec, gn, cos, sa, sb)


def _residual_epilogue(y, x_ref, mc_ref, mb_ref, g_ref, is_ctx, x_out_ref, h_out_ref):
    x_new = x_ref[...] + _mod_row(is_ctx, mc_ref, mb_ref, 0) * _rms(y, g_ref[0:1, :])
    x_out_ref[...] = x_new
    h = _rms(x_new, g_ref[1:2, :]) * (1.0 + _mod_row(is_ctx, mc_ref, mb_ref, 2)) + _mod_row(is_ctx, mc_ref, mb_ref, 1)
    h_out_ref[...] = h.astype(BF16)


def _out_proj_kernel(ya_ref, yb_ref, yc_ref, yd_ref, w_ref, x_ref, mc_ref, mb_ref, g_ref, x_out_ref, h_out_ref,
                     *, tm, ctx_len):
    gw = GROUP_W
    y = (jnp.dot(ya_ref[...], w_ref[0:gw, :], preferred_element_type=F32)
         + jnp.dot(yb_ref[...], w_ref[gw:2 * gw, :], preferred_element_type=F32)
         + jnp.dot(yc_ref[...], w_ref[2 * gw:3 * gw, :], preferred_element_type=F32)
         + jnp.dot(yd_ref[...], w_ref[3 * gw:, :], preferred_element_type=F32))
    is_ctx = _ctx_rows(pl.program_id(1), tm, ctx_len)
    _residual_epilogue(y, x_ref, mc_ref, mb_ref, g_ref, is_ctx, x_out_ref, h_out_ref)


def _out_proj(ya, yb, yc, yd, w, x, mc3, mb3, g2, *, nb, tb, ctx_len):
    d = x.shape[1]
    gw = GROUP_W
    tm = _pick_tile(tb, 384)
    nt = tb // tm
    row = lambda b, t: (b * nt + t, 0)
    c2 = lambda b, t: (0, 0)
    return pl.pallas_call(
        functools.partial(_out_proj_kernel, tm=tm, ctx_len=ctx_len),
        grid=(nb, nt),
        in_specs=[pl.BlockSpec((tm, gw), row)] * 4
        + [pl.BlockSpec((4 * gw, d), c2), pl.BlockSpec((tm, d), row),
           pl.BlockSpec((3, d), c2), pl.BlockSpec((1, 3, d), lambda b, t: (b, 0, 0)), pl.BlockSpec((2, d), c2)],
        out_specs=[pl.BlockSpec((tm, d), row), pl.BlockSpec((tm, d), row)],
        out_shape=[jax.ShapeDtypeStruct(x.shape, F32), jax.ShapeDtypeStruct(x.shape, BF16)],
        compiler_params=_cparams("arbitrary", "arbitrary"),
    )(ya, yb, yc, yd, w, x, mc3, mb3, g2)


def _ffn_up_kernel(h_ref, wg_ref, wv_ref, cg_ref, cv_ref, o_ref, *, ctx_len):
    h = h_ref[...]
    starts, ends = _seq_edges(h.shape[0], ctx_len)
    gate = _dwconv3_rows(jnp.dot(h, wg_ref[...], preferred_element_type=F32), cg_ref, starts, ends)
    val = _dwconv3_rows(jnp.dot(h, wv_ref[...], preferred_element_type=F32), cv_ref, starts, ends)
    o_ref[...] = (_silu(gate) * val).astype(o_ref.dtype)


def _ffn_up(h, w_up, w_conv, *, nb, tb, ctx_len):
    r_tot, d = h.shape
    dff = w_up.shape[1] // 2
    tn = 256
    nj = dff // tn
    return pl.pallas_call(
        functools.partial(_ffn_up_kernel, ctx_len=ctx_len),
        grid=(nb, nj),
        in_specs=[pl.BlockSpec((tb, d), lambda b, j: (b, 0)),
                  pl.BlockSpec((d, tn), lambda b, j: (0, j)),
                  pl.BlockSpec((d, tn), lambda b, j: (0, nj + j)),
                  pl.BlockSpec((3, tn), lambda b, j: (0, j)),
                  pl.BlockSpec((3, tn), lambda b, j: (0, nj + j))],
        out_specs=pl.BlockSpec((tb, tn), lambda b, j: (b, j)),
        out_shape=jax.ShapeDtypeStruct((r_tot, dff), BF16),
        compiler_params=_cparams("arbitrary", "arbitrary"),
    )(h, w_up, w_up, w_conv, w_conv)


def _ffn_down_kernel(a_ref, w_ref, x_ref, mc_ref, mb_ref, g_ref, x_out_ref, h_out_ref, acc_ref, *, tm, ctx_len, nk):
    kk = pl.program_id(2)

    @pl.when(kk == 0)
    def _():
        acc_ref[...] = jnp.zeros_like(acc_ref)

    acc_ref[...] += jnp.dot(a_ref[...], w_ref[...], preferred_element_type=F32)

    @pl.when(kk == nk - 1)
    def _():
        is_ctx = _ctx_rows(pl.program_id(1), tm, ctx_len)
        _residual_epilogue(acc_ref[...], x_ref, mc_ref, mb_ref, g_ref, is_ctx, x_out_ref, h_out_ref)


def _ffn_down(act, w, x, mc3, mb3, g2, *, nb, tb, ctx_len):
    d = x.shape[1]
    dff = act.shape[1]
    tm = _pick_tile(tb, 576)
    nt = tb // tm
    tk = 512
    nk = dff // tk
    row = lambda b, t, k: (b * nt + t, 0)
    c2 = lambda b, t, k: (0, 0)
    return pl.pallas_call(
        functools.partial(_ffn_down_kernel, tm=tm, ctx_len=ctx_len, nk=nk),
        grid=(nb, nt, nk),
        in_specs=[pl.BlockSpec((tm, tk), lambda b, t, k: (b * nt + t, k)),
                  pl.BlockSpec((tk, d), lambda b, t, k: (k, 0)),
                  pl.BlockSpec((tm, d), row),
                  pl.BlockSpec((3, d), c2), pl.BlockSpec((1, 3, d), lambda b, t, k: (b, 0, 0)), pl.BlockSpec((2, d), c2)],
        out_specs=[pl.BlockSpec((tm, d), row), pl.BlockSpec((tm, d), row)],
        out_shape=[jax.ShapeDtypeStruct(x.shape, F32), jax.ShapeDtypeStruct(x.shape, BF16)],
        scratch_shapes=[pltpu.VMEM((tm, d), F32)],
        compiler_params=_cparams("arbitrary", "arbitrary", "arbitrary"),
    )(act, w, x, mc3, mb3, g2)


def _pad_cols(w, n):
    return jnp.pad(w, ((0, 0), (0, n - w.shape[1])))


def _layout_w_in(w_in):
    gw = GROUP_W
    a = w_in[:, :RWKV_IN]
    o = 3 * gw
    wa = jnp.concatenate([a[:, :o],
                          _pad_cols(a[:, o:o + RWKV_DECAY_LORA], RWKV_LORA_PAD),
                          _pad_cols(a[:, o + RWKV_DECAY_LORA:o + RWKV_DECAY_LORA + RWKV_ICLR_LORA], RWKV_LORA_PAD),
                          a[:, o + RWKV_DECAY_LORA + RWKV_ICLR_LORA:]], axis=1)
    wb = _pad_cols(w_in[:, RWKV_IN:RWKV_IN + MLA_IN], MLA_IN_PAD)
    wc = w_in[:, RWKV_IN + MLA_IN:RWKV_IN + MLA_IN + CONV_IN]
    wd = w_in[:, RWKV_IN + MLA_IN + CONV_IN:]
    return wa.astype(BF16), wb.astype(BF16), wc.astype(BF16), wd.astype(BF16)


def _layout_rwkv_shift(shift):
    o = 3 * GROUP_W
    return jnp.concatenate([shift[:, :o],
                            _pad_cols(shift[:, o:o + RWKV_DECAY_LORA], RWKV_LORA_PAD),
                            _pad_cols(shift[:, o + RWKV_DECAY_LORA:o + RWKV_DECAY_LORA + RWKV_ICLR_LORA], RWKV_LORA_PAD),
                            shift[:, o + RWKV_DECAY_LORA + RWKV_ICLR_LORA:]], axis=1)


def _pad_lora_rows(w):
    return jnp.pad(w, ((0, 0), (0, RWKV_LORA_PAD - w.shape[1]), (0, 0))).astype(BF16)


def _layout_mla(w_uq, w_ukv):
    hp = MLA_HEAD_PAD
    dqk = MLA_NOPE + MLA_ROPE
    wq = jnp.concatenate([_pad_cols(w_uq[:, h * dqk:(h + 1) * dqk], hp) for h in range(MLA_HEADS)], axis=1)
    dkv = MLA_NOPE + MLA_V
    wk = jnp.concatenate([_pad_cols(w_ukv[:, h * dkv:h * dkv + MLA_NOPE], hp) for h in range(MLA_HEADS)], axis=1)
    wv = jnp.concatenate([w_ukv[:, h * dkv + MLA_NOPE:(h + 1) * dkv] for h in range(MLA_HEADS)], axis=1)
    return wq.astype(BF16), wk.astype(BF16), wv.astype(BF16)


def _rope_tables(seq_len, ctx_len, rot_dim, lead, width):
    rows = seq_len // GRID_W
    half = rot_dim // 2
    quarter = half // 2
    inv = ROPE_BASE ** (-jnp.arange(0, half, 2, dtype=F32) / half)
    row = jnp.repeat(jnp.arange(rows, dtype=F32), GRID_W)
    col = jnp.tile(jnp.arange(GRID_W, dtype=F32), rows)
    zeros = jnp.zeros((seq_len, quarter), F32)
    cos_parts, sa_parts, sb_parts = [], [], []
    for pos in (row, col):
        ang = pos[:, None] * inv[None, :]
        c, s = jnp.cos(ang), jnp.sin(ang)
        cos_parts += [c, c]
        sa_parts += [-s, zeros]
        sb_parts += [zeros, s]

    def table(parts, fill):
        body = jnp.concatenate(parts, axis=1)
        body = jnp.concatenate([jnp.full((seq_len, lead), fill, F32), body,
                                jnp.full((seq_len, width - lead - rot_dim), fill, F32)], axis=1)
        return jnp.concatenate([jnp.full((ctx_len, width), fill, F32), body], axis=0)

    return table(cos_parts, 1.0), table(sa_parts, 0.0), table(sb_parts, 0.0)


def _head_sum_matrix(width, head):
    i = jnp.arange(width) // head
    return (i[:, None] == i[None, :]).astype(BF16)


def _chunk_tri(tm, chunk):
    i = jnp.arange(tm)
    same = (i[:, None] // chunk) == (i[None, :] // chunk)
    lower = jnp.logical_and(same, i[None, :] <= i[:, None])
    upper = jnp.logical_and(same, i[None, :] >= i[:, None])
    return jnp.stack([lower, upper]).astype(BF16)


def _token_mixer(h, lw, tables, *, nb, tb, ctx_len):
    kw = dict(nb=nb, tb=tb)
    wa, wb, wc, wd = lw["w_in"]
    pa = _matmul(h, wa, F32, tn=512, **kw)
    pb = _matmul(h, wb, BF16, tn=MLA_IN_PAD, **kw)
    pc = _matmul(h, wc, BF16, tn=512, **kw)
    pd = _matmul(h, wd, BF16, tn=512, **kw)

    at, rt, bt, kt, wcum, v, gate, bonus = _rwkv_features(
        pa, lw["rwkv_shift"], lw["rwkv_w0"], lw["rwkv_w_up"], lw["rwkv_a0"], lw["rwkv_a_up"], lw["rwkv_g_up"],
        lw["rwkv_vecs"], tables["ones_bd"], tables["tri"], ctx_len=ctx_len, **kw)
    y_scan = _rwkv_scan(at, rt, bt, kt, wcum, v, ctx_len=ctx_len, feat_tm=256, **kw)
    ya = _rwkv_finish(y_scan, gate, bonus, lw["rwkv_vecs"], tables["ones_bd"], **kw)

    wq, wk, wv = lw["mla_w"]
    q, k, vv = _mla_project(pb, lw["mla_q_norm"], lw["mla_kv_norm"], wq, wk, wv, *tables["mla_rope"], **kw)
    yb = _mla_attention(q, k, vv, ctx_len=ctx_len, **kw)

    yc = _conv_mixer(pc, lw["conv_w"], ctx_len=ctx_len, **kw)
    yd = _retention(pd, lw["ret_decay"], lw["ret_gn_g"], *tables["ret_rope"], ctx_len=ctx_len, **kw)
    return ya, yb, yc, yd


def kernel(x, c, ctx, c_ctx, mod_w, mod_b, norm_g, w_in, rwkv_shift, rwkv_w0, rwkv_w_up, rwkv_a0, rwkv_a_up,
           rwkv_g_up, rwkv_vecs, mla_q_norm, mla_kv_norm, mla_w_uq, mla_w_ukv, conv_w, ret_decay, ret_gn_g, w_out,
           mlp_w_up, mlp_conv, mlp_w_down):
    nb, seq_len, d = x.shape
    ctx_len = ctx.shape[1]
    depth = mod_w.shape[0]
    tb = ctx_len + seq_len
    assert nb + 1 <= 8 and d == D_MODEL
    kw = dict(nb=nb, tb=tb, ctx_len=ctx_len)

    tables = {
        "ones_bd": _head_sum_matrix(GROUP_W, RWKV_HEAD),
        "tri": _chunk_tri(256, RWKV_CHUNK),
        "mla_rope": _rope_tables(seq_len, ctx_len, MLA_ROPE, MLA_NOPE, MLA_HEAD_PAD),
        "ret_rope": _rope_tables(seq_len, ctx_len, RET_HEAD, 0, RET_HEAD),
    }

    c_pad = jnp.concatenate([c, c_ctx[None, :], jnp.zeros((8 - nb - 1, d), F32)], axis=0)
    mods = _modulation(c_pad, mod_w, mod_b).reshape(depth, 8, N_MOD, d)
    m_lat = mods[:, :nb]
    m_ctx = mods[:, nb]

    xs = jnp.concatenate([ctx, x], axis=1).reshape(nb * tb, d)
    h = _prologue(xs, m_ctx[0], m_lat[0], norm_g[0, 0:1], **kw)

    for l in range(depth):
        lw = {
            "w_in": _layout_w_in(w_in[l]),
            "rwkv_shift": _layout_rwkv_shift(rwkv_shift[l]),
            "rwkv_w0": rwkv_w0[l], "rwkv_w_up": _pad_lora_rows(rwkv_w_up[l]),
            "rwkv_a0": rwkv_a0[l], "rwkv_a_up": _pad_lora_rows(rwkv_a_up[l]),
            "rwkv_g_up": rwkv_g_up[l].astype(BF16), "rwkv_vecs": rwkv_vecs[l],
            "mla_q_norm": mla_q_norm[l][None, :], "mla_kv_norm": mla_kv_norm[l][None, :],
            "mla_w": _layout_mla(mla_w_uq[l], mla_w_ukv[l]),
            "conv_w": conv_w[l],
            "ret_decay": jnp.broadcast_to(
                jnp.pad(ret_decay[l].T, ((0, 0), (0, 6)))[:, :, None], (RET_HEADS, 8, 128)),
            "ret_gn_g": ret_gn_g[l][None, :],
        }
        ya, yb, yc, yd = _token_mixer(h, lw, tables, **kw)

        xs, h = _out_proj(ya, yb, yc, yd, w_out[l].astype(BF16), xs,
                          m_ctx[l][jnp.array([2, 3, 4])], m_lat[l][:, jnp.array([2, 3, 4])],
                          norm_g[l, 1:3], **kw)
        act = _ffn_up(h, mlp_w_up[l].astype(BF16), mlp_conv[l], **kw)
        nl = min(l + 1, depth - 1)
        mc3 = jnp.stack([m_ctx[l][5], m_ctx[nl][0], m_ctx[nl][1]])
        mb3 = jnp.stack([m_lat[l][:, 5], m_lat[nl][:, 0], m_lat[nl][:, 1]], axis=1)
        g2 = jnp.stack([norm_g[l, 3], norm_g[nl, 0]])
        xs, h = _ffn_down(act, mlp_w_down[l].astype(BF16), xs, mc3, mb3, g2, **kw)

    return xs.reshape(nb, tb, d)[:, ctx_len:]
```

```python
import functools
import math

import jax
import jax.numpy as jnp
from jax import lax
from jax.experimental import pallas as pl
from jax.experimental.pallas import tpu as pltpu

F32 = jnp.float32
BF16 = jnp.bfloat16

D_MODEL = 2048
GRID_W = 64
GROUP_W = 512
N_MOD = 6
NORM_EPS = 1e-6
ROPE_BASE = 10000.0

RWKV_HEAD = 64
RWKV_HEADS = GROUP_W // RWKV_HEAD
RWKV_DECAY_LORA = 96
RWKV_ICLR_LORA = 96
RWKV_GATE_LORA = 256
RWKV_IN = 3 * GROUP_W + RWKV_DECAY_LORA + RWKV_ICLR_LORA + RWKV_GATE_LORA
RWKV_LORA_PAD = 128
RWKV_IN_PAD = 3 * GROUP_W + 2 * RWKV_LORA_PAD + RWKV_GATE_LORA
RWKV_DECAY_SCALE = math.exp(-0.5)
RWKV_GN_EPS = 64e-5
RWKV_CHUNK = 64

MLA_HEADS = 4
MLA_NOPE = 128
MLA_ROPE = 64
MLA_V = 128
MLA_Q_RANK = 384
MLA_KV_RANK = 256
MLA_IN = MLA_Q_RANK + MLA_KV_RANK + MLA_ROPE
MLA_IN_PAD = 768
MLA_HEAD_PAD = 256
MLA_SCALE = (MLA_NOPE + MLA_ROPE) ** -0.5

CONV_IN = 3 * GROUP_W

RET_HEADS = 4
RET_HEAD = 128
RET_IN = 4 * GROUP_W
GN_EPS = 1e-5

D_FF = 5632
VMEM_LIMIT = 56 * 1024 * 1024


def _cparams(*sem):
    return pltpu.CompilerParams(dimension_semantics=sem, vmem_limit_bytes=VMEM_LIMIT)


def _pick_tile(n, target, mult=16):
    best = None
    for t in range(mult, min(n, target) + 1, mult):
        if n % t == 0:
            best = t
    assert best is not None, (n, target)
    return best


def _dot(a, b):
    return jnp.dot(a.astype(BF16), b.astype(BF16), preferred_element_type=F32)


def _dot_nt(a, b):
    return lax.dot_general(a.astype(BF16), b.astype(BF16), (((1,), (1,)), ((), ())),
                           preferred_element_type=F32)


def _dot_tn(a, b):
    return lax.dot_general(a.astype(BF16), b.astype(BF16), (((0,), (0,)), ((), ())),
                           preferred_element_type=F32)


def _split3(x):
    hi = x.astype(BF16)
    r1 = x - hi.astype(F32)
    mid = r1.astype(BF16)
    lo = (r1 - mid.astype(F32)).astype(BF16)
    return hi, mid, lo


def _dot_wide_rhs(m, x):
    hi, mid, lo = _split3(x)
    return (jnp.dot(m, hi, preferred_element_type=F32) + jnp.dot(m, mid, preferred_element_type=F32)
            + jnp.dot(m, lo, preferred_element_type=F32))


def _dot_wide_lhs(x, m):
    hi, mid, lo = _split3(x)
    return (jnp.dot(hi, m, preferred_element_type=F32) + jnp.dot(mid, m, preferred_element_type=F32)
            + jnp.dot(lo, m, preferred_element_type=F32))


def _sigmoid(x):
    return 1.0 / (1.0 + jnp.exp(-x))


def _silu(x):
    return x * _sigmoid(x)


def _rms(x, g):
    ms = jnp.mean(x * x, axis=-1, keepdims=True)
    return x * lax.rsqrt(ms + NORM_EPS) * g


def _ctx_rows(tile_idx, tm, ctx_len):
    rows = tile_idx * tm + lax.broadcasted_iota(jnp.int32, (tm, 1), 0)
    return rows < ctx_len


def _mod_row(is_ctx, mc_ref, mb_ref, k):
    return jnp.where(is_ctx, mc_ref[k:k + 1, :], mb_ref[0, k:k + 1, :])


def _shift_rows(u, first_row, last_row):
    n = u.shape[0]
    rows = lax.broadcasted_iota(jnp.int32, (n, 1), 0)
    prev = jnp.where(rows == 0, first_row, pltpu.roll(u, 1, 0))
    nxt = jnp.where(rows == n - 1, last_row, pltpu.roll(u, n - 1, 0))
    return prev, nxt


def _rope(x, cos, sa, sb, half):
    n = x.shape[-1]
    return x * cos + pltpu.roll(x, n - half, 1) * sa + pltpu.roll(x, half, 1) * sb


def _mod_kernel(c_ref, w_ref, b_ref, o_ref):
    o_ref[0] = _dot(_silu(c_ref[...]), w_ref[0]) + b_ref[0]


def _modulation(c_pad, mod_w, mod_b):
    depth, d, n = mod_w.shape
    tn = 1024
    return pl.pallas_call(
        _mod_kernel,
        grid=(depth, n // tn),
        in_specs=[pl.BlockSpec((8, d), lambda l, j: (0, 0)),
                  pl.BlockSpec((1, d, tn), lambda l, j: (l, 0, j)),
                  pl.BlockSpec((1, 1, tn), lambda l, j: (l, 0, j))],
        out_specs=pl.BlockSpec((1, 8, tn), lambda l, j: (l, 0, j)),
        out_shape=jax.ShapeDtypeStruct((depth, 8, n), F32),
        compiler_params=_cparams("arbitrary", "arbitrary"),
    )(c_pad, mod_w, mod_b.reshape(depth, 1, n))


def _prologue_kernel(x_ref, mc_ref, mb_ref, g_ref, h_ref, *, tm, ctx_len):
    is_ctx = _ctx_rows(pl.program_id(1), tm, ctx_len)
    h = _rms(x_ref[...], g_ref[...]) * (1.0 + _mod_row(is_ctx, mc_ref, mb_ref, 1)) + _mod_row(is_ctx, mc_ref, mb_ref, 0)
    h_ref[...] = h.astype(BF16)


def _prologue(x, mc, mb, g, *, nb, tb, ctx_len):
    d = x.shape[1]
    tm = _pick_tile(tb, 768)
    nt = tb // tm
    return pl.pallas_call(
        functools.partial(_prologue_kernel, tm=tm, ctx_len=ctx_len),
        grid=(nb, nt),
        in_specs=[pl.BlockSpec((tm, d), lambda b, t: (b * nt + t, 0)),
                  pl.BlockSpec((N_MOD, d), lambda b, t: (0, 0)),
                  pl.BlockSpec((1, N_MOD, d), lambda b, t: (b, 0, 0)),
                  pl.BlockSpec((1, d), lambda b, t: (0, 0))],
        out_specs=pl.BlockSpec((tm, d), lambda b, t: (b * nt + t, 0)),
        out_shape=jax.ShapeDtypeStruct(x.shape, BF16),
        compiler_params=_cparams("arbitrary", "arbitrary"),
    )(x, mc, mb, g)


def _matmul_kernel(a_ref, w_ref, o_ref):
    o_ref[...] = jnp.dot(a_ref[...], w_ref[...], preferred_element_type=F32).astype(o_ref.dtype)


def _matmul(a, w, out_dtype, *, nb, tb, tn):
    k, n = w.shape
    assert n % tn == 0
    return pl.pallas_call(
        _matmul_kernel,
        grid=(nb, n // tn),
        in_specs=[pl.BlockSpec((tb, k), lambda b, j: (b, 0)),
                  pl.BlockSpec((k, tn), lambda b, j: (0, j))],
        out_specs=pl.BlockSpec((tb, tn), lambda b, j: (b, j)),
        out_shape=jax.ShapeDtypeStruct((a.shape[0], n), out_dtype),
        compiler_params=_cparams("arbitrary", "arbitrary"),
    )(a, w)


def _rwkv_feat_kernel(p_ref, pprev_ref, pnext_ref, shift_ref, w0_ref, wup_ref, a0_ref, aup_ref, gup_ref,
                      vecs_ref, ones_ref, tri_ref,
                      at_ref, rt_ref, bt_ref, kt_ref, wc_ref, v_ref, gate_ref, bonus_ref,
                      *, tm, ctx_tiles, nt):
    t = pl.program_id(1)
    first = jnp.logical_or(t == 0, t == ctx_tiles)
    last = jnp.logical_or(t == ctx_tiles - 1, t == nt - 1)
    u = p_ref[...]
    halo_prev = jnp.where(first, 0.0, pprev_ref[7:8, :])
    halo_next = jnp.where(last, 0.0, pnext_ref[0:1, :])
    prev, nxt = _shift_rows(u, halo_prev, halo_next)
    p = u + shift_ref[0:1, :] * (prev - u) + shift_ref[1:2, :] * (nxt - u)

    gw = GROUP_W
    r = p[:, 0:gw]
    k = p[:, gw:2 * gw]
    v = p[:, 2 * gw:3 * gw]
    wd = p[:, 3 * gw:3 * gw + RWKV_LORA_PAD]
    ad = p[:, 3 * gw + RWKV_LORA_PAD:3 * gw + 2 * RWKV_LORA_PAD]
    gd = p[:, 3 * gw + 2 * RWKV_LORA_PAD:]
    k_k = vecs_ref[0:1, :]
    k_a = vecs_ref[1:2, :]
    r_k = vecs_ref[2:3, :]
    ones = ones_ref[...]

    kk = k * k_k
    kk = kk * lax.rsqrt(_dot_wide_lhs(kk * kk, ones) + 1e-12)
    v_ref[...] = v.astype(BF16)
    gate_ref[...] = _dot(_sigmoid(gd), gup_ref[...])
    bonus_ref[...] = _dot_wide_lhs(r * k * r_k, ones) * v

    tanh_wd = jnp.tanh(wd)
    nch = tm // RWKV_CHUNK
    for d in range(2):
        lw = -RWKV_DECAY_SCALE * _sigmoid(w0_ref[d:d + 1, :] + _dot(tanh_wd, wup_ref[d]))
        l_inc = _dot_wide_rhs(tri_ref[d], lw)
        l_exc = l_inc - lw
        asig = _sigmoid(a0_ref[d:d + 1, :] + _dot(ad, aup_ref[d]))
        k_d = k * (1.0 + (asig - 1.0) * k_a)
        e_inc = jnp.exp(l_inc)
        e_neg = jnp.exp(-l_inc)
        at_ref[d] = (-kk * jnp.exp(l_exc)).astype(BF16)
        rt_ref[d] = (r * e_inc).astype(BF16)
        bt_ref[d] = (kk * asig * e_neg).astype(BF16)
        kt_ref[d] = (k_d * e_neg).astype(BF16)
        end = RWKV_CHUNK - 1 if d == 0 else 0
        rows = [e_inc[c * RWKV_CHUNK + end:c * RWKV_CHUNK + end + 1, :] for c in range(nch)]
        rows.append(jnp.zeros((8 - nch, gw), F32))
        wc_ref[d] = jnp.concatenate(rows, axis=0)


def _rwkv_features(pa, shift, w0, wup, a0, aup, gup, vecs, ones_bd, tri, *, nb, tb, ctx_len):
    r_tot = pa.shape[0]
    tm = 256
    assert ctx_len % tm == 0 and tb % tm == 0
    nt = tb // tm
    hb = tm // 8
    nblk8 = r_tot // 8
    gw = GROUP_W
    row = lambda b, t: (b * nt + t, 0)
    drow = lambda b, t: (0, b * nt + t, 0)
    const2 = lambda b, t: (0, 0)
    const3 = lambda b, t: (0, 0, 0)
    feat = jax.ShapeDtypeStruct((2, r_tot, gw), BF16)
    return pl.pallas_call(
        functools.partial(_rwkv_feat_kernel, tm=tm, ctx_tiles=ctx_len // tm, nt=nt),
        grid=(nb, nt),
        in_specs=[pl.BlockSpec((tm, RWKV_IN_PAD), row),
                  pl.BlockSpec((8, RWKV_IN_PAD), lambda b, t: (jnp.maximum((b * nt + t) * hb - 1, 0), 0)),
                  pl.BlockSpec((8, RWKV_IN_PAD), lambda b, t: (jnp.minimum((b * nt + t + 1) * hb, nblk8 - 1), 0)),
                  pl.BlockSpec((2, RWKV_IN_PAD), const2),
                  pl.BlockSpec((2, gw), const2),
                  pl.BlockSpec((2, RWKV_LORA_PAD, gw), const3),
                  pl.BlockSpec((2, gw), const2),
                  pl.BlockSpec((2, RWKV_LORA_PAD, gw), const3),
                  pl.BlockSpec((RWKV_GATE_LORA, gw), const2),
                  pl.BlockSpec((5, gw), const2),
                  pl.BlockSpec((gw, gw), const2),
                  pl.BlockSpec((2, tm, tm), const3)],
        out_specs=[pl.BlockSpec((2, tm, gw), drow)] * 4
        + [pl.BlockSpec((2, 8, gw), drow),
           pl.BlockSpec((tm, gw), row), pl.BlockSpec((tm, gw), row), pl.BlockSpec((tm, gw), row)],
        out_shape=[feat, feat, feat, feat,
                   jax.ShapeDtypeStruct((2, r_tot // tm * 8, gw), F32),
                   jax.ShapeDtypeStruct((r_tot, gw), BF16),
                   jax.ShapeDtypeStruct((r_tot, gw), F32),
                   jax.ShapeDtypeStruct((r_tot, gw), F32)],
        compiler_params=_cparams("arbitrary", "arbitrary"),
    )(pa, pa, pa, shift, w0, wup, a0, aup, gup, vecs, ones_bd, tri)


RWKV_QUAD = 4 * RWKV_HEAD
RWKV_INV_LEVELS = 6
M_STRICT, M_INCL, M_LEVEL0, M_EYE, M_SAME = 0, 1, 2, 2 + RWKV_INV_LEVELS, 3 + RWKV_INV_LEVELS
RWKV_N_MASKS = 4 + RWKV_INV_LEVELS


def _rwkv_scan_masks():
    n, c = RWKV_QUAD, RWKV_CHUNK
    r = jnp.arange(n)[:, None]
    col = jnp.arange(n)[None, :]
    same = (r // c) == (col // c)
    t, j = r % c, col % c
    out = []
    for d in range(2):
        before = (j < t) if d == 0 else (j > t)
        ms = [same & before, same & (before | (j == t))]
        for lvl in range(RWKV_INV_LEVELS):
            s = 2 ** lvl
            blk = (r // (2 * s)) == (col // (2 * s))
            late_r = (r % (2 * s)) >= s
            late_c = (col % (2 * s)) >= s
            ms.append(blk & ((late_r & ~late_c) if d == 0 else (~late_r & late_c)))
        ms += [r == col, same]
        out.append(jnp.stack(ms))
    return jnp.stack(out).astype(F32)


def _head_stack(x):
    hd = RWKV_HEAD
    return jnp.concatenate([x[:, h * hd:(h + 1) * hd] for h in range(4)], axis=0)


def _head_unstack(x):
    c = RWKV_CHUNK
    return jnp.concatenate([x[h * c:(h + 1) * c, :] for h in range(4)], axis=1)


def _rwkv_scan_kernel(atf_ref, rtf_ref, btf_ref, ktf_ref, vf_ref, wcf_ref,
                      atb_ref, rtb_ref, btb_ref, ktb_ref, vb_ref, wcb_ref, mask_ref, same_ref,
                      yf_ref, yb_ref, s_ref, *, n_chunks, ctx_chunks, per_tile):
    i = pl.program_id(1)

    @pl.when(i == 0)
    def _():
        s_ref[...] = jnp.zeros_like(s_ref)

    c_bwd = jnp.where(i < ctx_chunks, ctx_chunks - 1 - i, n_chunks - 1 + ctx_chunks - i)
    wrow = (i % per_tile, c_bwd % per_tile)
    ins = ((atf_ref, rtf_ref, btf_ref, ktf_ref, vf_ref, wcf_ref, yf_ref),
           (atb_ref, rtb_ref, btb_ref, ktb_ref, vb_ref, wcb_ref, yb_ref))
    qw = RWKV_QUAD
    n4 = 4 * RWKV_CHUNK
    chains = [(d, q) for d in range(2) for q in range(GROUP_W // qw)]

    def masked_stack(x):
        return jnp.concatenate([x, x, x, x], axis=0) * same_ref[...]

    def block_diag(x_st, d, m):
        return jnp.concatenate([x_st, x_st, x_st, x_st], axis=1) * mask_ref[d, m]

    st = {}
    for ch in chains:
        d, q = ch
        a_ref, r_ref, b_ref, k_ref, v_ref, _, _ = ins[d]
        sl = slice(q * qw, (q + 1) * qw)
        b_, k_ = b_ref[0, :, sl], k_ref[0, :, sl]
        st[ch] = dict(ms_ar=jnp.concatenate([masked_stack(a_ref[0, :, sl]), masked_stack(r_ref[0, :, sl])], axis=0),
                      ms_b=masked_stack(b_), ms_k=masked_stack(k_), b=b_, k=k_, v_st=_head_stack(v_ref[:, sl]))
    for ch in chains:
        e = st[ch]
        e["aa_b"] = _dot_nt(e["ms_ar"], e["b"])
        e["aa_k"] = _dot_nt(e["ms_ar"], e["k"])
    for ch in chains:
        d = ch[0]
        e = st[ch]
        e["a_ab"] = block_diag(e["aa_b"][:n4], d, M_STRICT)
        e["a_ak"] = block_diag(e["aa_k"][:n4], d, M_STRICT)
        e["a_rb"] = block_diag(e["aa_b"][n4:], d, M_INCL)
        e["a_rk"] = block_diag(e["aa_k"][n4:], d, M_INCL)
        e["t"] = mask_ref[d, M_EYE] + e["a_ab"] * mask_ref[d, M_LEVEL0]
    for lvl in range(1, RWKV_INV_LEVELS):
        for ch in chains:
            e = st[ch]
            e["w"] = _dot(e["a_ab"] * mask_ref[ch[0], M_LEVEL0 + lvl], e["t"])
        for ch in chains:
            e = st[ch]
            e["t"] = e["t"] + _dot(e["t"], e["w"])
    for ch in chains:
        e = st[ch]
        e["akv"] = _dot(e["a_ak"], e["v_st"])
        e["rkv"] = _dot(e["a_rk"], e["v_st"])
        e["vtk"] = _dot_tn(e["v_st"], e["ms_k"])
    for ch in chains:
        d, q = ch
        e = st[ch]
        e["s"] = s_ref[d, :, q * qw:(q + 1) * qw]
        e["xs"] = _dot_nt(e["ms_ar"], e["s"])
    for ch in chains:
        e = st[ch]
        e["u"] = _dot(e["t"], e["xs"][:n4] + e["akv"])
    for ch in chains:
        d, q = ch
        e = st[ch]
        sl = slice(q * qw, (q + 1) * qw)
        y_st = e["xs"][n4:] + _dot(e["a_rb"], e["u"]) + e["rkv"]
        ins[d][6][:, sl] = _head_unstack(y_st)
        wc = ins[d][5][0, pl.ds(wrow[d], 1), sl]
        s_ref[d, :, sl] = (e["s"] + _dot_tn(e["u"], e["ms_b"]) + e["vtk"]) * wc


def _rwkv_scan(at, rt, bt, kt, wc, v, masks, same, *, nb, tb, ctx_len, feat_tm):
    r_tot = v.shape[0]
    gw = GROUP_W
    c = RWKV_CHUNK
    n_chunks = tb // c
    ctx_chunks = ctx_len // c
    per_tile = feat_tm // c

    def cf(b, i):
        return b * n_chunks + i

    def cb(b, i):
        return b * n_chunks + jnp.where(i < ctx_chunks, ctx_chunks - 1 - i, n_chunks - 1 + ctx_chunks - i)

    def specs(d, cidx):
        feat = pl.BlockSpec((1, c, gw), lambda b, i: (d, cidx(b, i), 0))
        return [feat, feat, feat, feat,
                pl.BlockSpec((c, gw), lambda b, i: (cidx(b, i), 0)),
                pl.BlockSpec((1, 8, gw), lambda b, i: (d, cidx(b, i) // per_tile, 0))]

    return pl.pallas_call(
        functools.partial(_rwkv_scan_kernel, n_chunks=n_chunks, ctx_chunks=ctx_chunks, per_tile=per_tile),
        grid=(nb, n_chunks),
        in_specs=specs(0, cf) + specs(1, cb)
        + [pl.BlockSpec((2, RWKV_N_MASKS, RWKV_QUAD, RWKV_QUAD), lambda b, i: (0, 0, 0, 0)),
           pl.BlockSpec((RWKV_QUAD, RWKV_QUAD), lambda b, i: (0, 0))],
        out_specs=[pl.BlockSpec((c, gw), lambda b, i: (cf(b, i), 0)),
                   pl.BlockSpec((c, gw), lambda b, i: (cb(b, i), 0))],
        out_shape=[jax.ShapeDtypeStruct((r_tot, gw), F32), jax.ShapeDtypeStruct((r_tot, gw), F32)],
        scratch_shapes=[pltpu.VMEM((2, RWKV_HEAD, gw), F32)],
        compiler_params=_cparams("arbitrary", "arbitrary"),
    )(at, rt, bt, kt, v, wc, at, rt, bt, kt, v, wc, masks, same)


def _rwkv_finish_kernel(yf_ref, yb_ref, gate_ref, bonus_ref, vecs_ref, ones_ref, o_ref):
    ones = ones_ref[...]
    y = yf_ref[...] + yb_ref[...]
    inv = 1.0 / RWKV_HEAD
    mu = _dot_wide_lhs(y, ones) * inv
    yc = y - mu
    var = _dot_wide_lhs(yc * yc, ones) * inv
    out = yc * lax.rsqrt(var + RWKV_GN_EPS) * vecs_ref[3:4, :] + vecs_ref[4:5, :] + bonus_ref[...]
    o_ref[...] = (out * gate_ref[...]).astype(o_ref.dtype)


def _rwkv_finish(yf, yb, gate, bonus, vecs, ones_bd, *, nb, tb):
    r_tot = gate.shape[0]
    gw = GROUP_W
    tm = _pick_tile(tb, 768)
    nt = tb // tm
    row = lambda b, t: (b * nt + t, 0)
    return pl.pallas_call(
        _rwkv_finish_kernel,
        grid=(nb, nt),
        in_specs=[pl.BlockSpec((tm, gw), row), pl.BlockSpec((tm, gw), row),
                  pl.BlockSpec((tm, gw), row), pl.BlockSpec((tm, gw), row),
                  pl.BlockSpec((5, gw), lambda b, t: (0, 0)),
                  pl.BlockSpec((gw, gw), lambda b, t: (0, 0))],
        out_specs=pl.BlockSpec((tm, gw), row),
        out_shape=jax.ShapeDtypeStruct((r_tot, gw), BF16),
        compiler_params=_cparams("arbitrary", "arbitrary"),
    )(yf, yb, gate, bonus, vecs, ones_bd)


def _mla_proj_kernel(p_ref, qn_ref, kvn_ref, wq_ref, wk_ref, wv_ref, cos_ref, sa_ref, sb_ref,
                     q_ref, k_ref, v_ref):
    p = p_ref[...].astype(F32)
    c_q = p[:, :MLA_Q_RANK]
    c_kv = p[:, MLA_Q_RANK:MLA_Q_RANK + MLA_KV_RANK]
    k_r = p[:, MLA_Q_RANK + MLA_KV_RANK:]
    n_q = _rms(c_q, qn_ref[...])
    n_kv = _rms(c_kv, kvn_ref[...])
    q = _dot(n_q, wq_ref[...])
    kx = _dot(n_kv, wk_ref[...])
    v_ref[...] = _dot(n_kv, wv_ref[...]).astype(BF16)
    cos, sa, sb = cos_ref[...], sa_ref[...], sb_ref[...]
    kr_blk = jnp.concatenate([jnp.zeros_like(k_r), k_r], axis=1)
    hp = MLA_HEAD_PAD
    for h in range(MLA_HEADS):
        sl = slice(h * hp, (h + 1) * hp)
        q_ref[:, sl] = _rope(q[:, sl], cos, sa, sb, MLA_ROPE // 4).astype(BF16)
        k_ref[:, sl] = _rope(kx[:, sl] + kr_blk, cos, sa, sb, MLA_ROPE // 4).astype(BF16)


def _mla_project(pb, qn, kvn, wq, wk, wv, cos, sa, sb, *, nb, tb):
    r_tot = pb.shape[0]
    tm = _pick_tile(tb, 768)
    nt = tb // tm
    row = lambda b, t: (b * nt + t, 0)
    trow = lambda b, t: (t, 0)
    c2 = lambda b, t: (0, 0)
    hw = MLA_HEADS * MLA_HEAD_PAD
    return pl.pallas_call(
        _mla_proj_kernel,
        grid=(nb, nt),
        in_specs=[pl.BlockSpec((tm, MLA_IN_PAD), row),
                  pl.BlockSpec((1, MLA_Q_RANK), c2), pl.BlockSpec((1, MLA_KV_RANK), c2),
                  pl.BlockSpec((MLA_Q_RANK, hw), c2), pl.BlockSpec((MLA_KV_RANK, hw), c2),
                  pl.BlockSpec((MLA_KV_RANK, GROUP_W), c2),
                  pl.BlockSpec((tm, MLA_HEAD_PAD), trow), pl.BlockSpec((tm, MLA_HEAD_PAD), trow),
                  pl.BlockSpec((tm, MLA_HEAD_PAD), trow)],
        out_specs=[pl.BlockSpec((tm, hw), row), pl.BlockSpec((tm, hw), row), pl.BlockSpec((tm, GROUP_W), row)],
        out_shape=[jax.ShapeDtypeStruct((r_tot, hw), BF16), jax.ShapeDtypeStruct((r_tot, hw), BF16),
                   jax.ShapeDtypeStruct((r_tot, GROUP_W), BF16)],
        compiler_params=_cparams("arbitrary", "arbitrary"),
    )(pb, qn, kvn, wq, wk, wv, cos, sa, sb)


def _softmax_pv(s, v):
    s = s * MLA_SCALE
    m = jnp.max(s, axis=-1, keepdims=True)
    e = jnp.exp(s - m)
    l = jnp.sum(e, axis=-1, keepdims=True)
    return _dot(e, v) / l


def _mla_attn_kernel(q_ref, k_ref, v_ref, o_ref, *, ctx_len, tq, n_qt):
    k_all = k_ref[...]
    v_all = v_ref[...]
    o_ref[0:ctx_len, :] = _softmax_pv(_dot_nt(q_ref[0:ctx_len, :], k_all[0:ctx_len]), v_all[0:ctx_len]).astype(BF16)

    def body(i, carry):
        row0 = pl.multiple_of(ctx_len + i * tq, tq)
        s = _dot_nt(q_ref[pl.ds(row0, tq), :], k_all)
        o_ref[pl.ds(row0, tq), :] = _softmax_pv(s, v_all).astype(BF16)
        return carry

    lax.fori_loop(0, n_qt, body, 0)


def _mla_attention(q, k, v, *, nb, tb, ctx_len):
    r_tot = q.shape[0]
    tq = 256
    assert (tb - ctx_len) % tq == 0 and ctx_len % tq == 0
    hp = MLA_HEAD_PAD
    return pl.pallas_call(
        functools.partial(_mla_attn_kernel, ctx_len=ctx_len, tq=tq, n_qt=(tb - ctx_len) // tq),
        grid=(nb, MLA_HEADS),
        in_specs=[pl.BlockSpec((tb, hp), lambda b, h: (b, h)),
                  pl.BlockSpec((tb, hp), lambda b, h: (b, h)),
                  pl.BlockSpec((tb, MLA_V), lambda b, h: (b, h))],
        out_specs=pl.BlockSpec((tb, MLA_V), lambda b, h: (b, h)),
        out_shape=jax.ShapeDtypeStruct((r_tot, GROUP_W), BF16),
        compiler_params=_cparams("arbitrary", "arbitrary"),
    )(q, k, v)


def _seq_edges(n, ctx_len):
    rows = lax.broadcasted_iota(jnp.int32, (n, 1), 0)
    starts = jnp.logical_or(rows == 0, rows == ctx_len)
    ends = jnp.logical_or(rows == ctx_len - 1, rows == n - 1)
    return starts, ends


def _dwconv3_rows(u, w_ref, starts, ends):
    n = u.shape[0]
    prev = jnp.where(starts, 0.0, pltpu.roll(u, 1, 0))
    nxt = jnp.where(ends, 0.0, pltpu.roll(u, n - 1, 0))
    return prev * w_ref[0:1, :] + u * w_ref[1:2, :] + nxt * w_ref[2:3, :]


def _conv_mix_kernel(b_ref, c_ref, u_ref, w_ref, o_ref, *, ctx_len):
    starts, ends = _seq_edges(b_ref.shape[0], ctx_len)
    z = c_ref[...].astype(F32) * u_ref[...].astype(F32)
    o_ref[...] = (b_ref[...].astype(F32) * _dwconv3_rows(z, w_ref, starts, ends)).astype(o_ref.dtype)


def _conv_mixer(pc, conv_w, *, nb, tb, ctx_len):
    r_tot = pc.shape[0]
    tc = 256
    nj = GROUP_W // tc
    return pl.pallas_call(
        functools.partial(_conv_mix_kernel, ctx_len=ctx_len),
        grid=(nb, nj),
        in_specs=[pl.BlockSpec((tb, tc), lambda b, j: (b, j)),
                  pl.BlockSpec((tb, tc), lambda b, j: (b, nj + j)),
                  pl.BlockSpec((tb, tc), lambda b, j: (b, 2 * nj + j)),
                  pl.BlockSpec((3, tc), lambda b, j: (0, j))],
        out_specs=pl.BlockSpec((tb, tc), lambda b, j: (b, j)),
        out_shape=jax.ShapeDtypeStruct((r_tot, GROUP_W), BF16),
        compiler_params=_cparams("arbitrary", "arbitrary"),
    )(pc, pc, pc, conv_w)


def _ret_kernel(q_ref, k_ref, v_ref, g_ref, dec_ref, gn_ref, cos_ref, sa_ref, sb_ref, o_ref, qs_ref, ks_ref,
                *, ctx_len, seq_len, tq):
    cos, sa, sb = cos_ref[...], sa_ref[...], sb_ref[...]
    qs_ref[...] = _rope(q_ref[...].astype(F32), cos, sa, sb, RET_HEAD // 4).astype(BF16)
    ks_ref[...] = _rope(k_ref[...].astype(F32), cos, sa, sb, RET_HEAD // 4).astype(BF16)
    lgf = -jnp.exp(dec_ref[0, 0:1, 0:1])
    lgb = -jnp.exp(dec_ref[0, 1:2, 0:1])
    gn = gn_ref[...]
    scale = RET_HEAD ** -0.5

    def finish(o, g):
        mu = jnp.mean(o, axis=-1, keepdims=True)
        oc = o - mu
        var = jnp.mean(oc * oc, axis=-1, keepdims=True)
        return _silu(g) * (oc * lax.rsqrt(var + GN_EPS) * gn)

    def bidir_decay(dist):
        fwd = dist >= 0
        ad = jnp.abs(dist).astype(F32)
        return jnp.exp(ad * jnp.where(fwd, lgf, lgb))

    k_c = ks_ref[0:ctx_len, :]
    v_c = v_ref[0:ctx_len, :]
    k_l = ks_ref[ctx_len:, :]
    v_l = v_ref[ctx_len:, :]

    ci = lax.broadcasted_iota(jnp.int32, (ctx_len, ctx_len), 0)
    cj = lax.broadcasted_iota(jnp.int32, (ctx_len, ctx_len), 1)
    s_c = _dot_nt(qs_ref[0:ctx_len, :], k_c) * bidir_decay(ci - cj)
    o_c = _dot(s_c, v_c) * scale
    o_ref[0:ctx_len, :] = finish(o_c, g_ref[0:ctx_len, :].astype(F32)).astype(o_ref.dtype)

    ri = lax.broadcasted_iota(jnp.int32, (tq, ctx_len), 0)
    rj = lax.broadcasted_iota(jnp.int32, (tq, ctx_len), 1)
    li = lax.broadcasted_iota(jnp.int32, (tq, seq_len), 0)
    lj = lax.broadcasted_iota(jnp.int32, (tq, seq_len), 1)

    def body(i, carry):
        row0 = pl.multiple_of(ctx_len + i * tq, tq)
        q = qs_ref[pl.ds(row0, tq), :]
        n_ctx = (i * tq + ri)
        d_ctx = (jnp.exp((n_ctx + ctx_len - rj).astype(F32) * lgf)
                 + jnp.exp((seq_len - n_ctx + rj).astype(F32) * lgb))
        s1 = _dot_nt(q, k_c) * d_ctx
        s2 = _dot_nt(q, k_l) * bidir_decay(i * tq + li - lj)
        o = (_dot(s1, v_c) + _dot(s2, v_l)) * scale
        o_ref[pl.ds(row0, tq), :] = finish(o, g_ref[pl.ds(row0, tq), :].astype(F32)).astype(o_ref.dtype)
        return carry

    lax.fori_loop(0, seq_len // tq, body, 0)


def _retention(pd, dec, gn, cos, sa, sb, *, nb, tb, ctx_len):
    r_tot = pd.shape[0]
    hd = RET_HEAD
    nh = RET_HEADS
    tq = 256
    seq_len = tb - ctx_len
    assert seq_len % tq == 0
    tbl = pl.BlockSpec((tb, hd), lambda b, h: (0, 0))
    return pl.pallas_call(
        functools.partial(_ret_kernel, ctx_len=ctx_len, seq_len=seq_len, tq=tq),
        grid=(nb, nh),
        in_specs=[pl.BlockSpec((tb, hd), lambda b, h: (b, h)),
                  pl.BlockSpec((tb, hd), lambda b, h: (b, nh + h)),
                  pl.BlockSpec((tb, hd), lambda b, h: (b, 2 * nh + h)),
                  pl.BlockSpec((tb, hd), lambda b, h: (b, 3 * nh + h)),
                  pl.BlockSpec((1, 8, 128), lambda b, h: (h, 0, 0)),
                  pl.BlockSpec((1, hd), lambda b, h: (0, h)),
                  tbl, tbl, tbl],
        out_specs=pl.BlockSpec((tb, hd), lambda b, h: (b, h)),
        out_shape=jax.ShapeDtypeStruct((r_tot, GROUP_W), BF16),
        scratch_shapes=[pltpu.VMEM((tb, hd), BF16), pltpu.VMEM((tb, hd), BF16)],
        compiler_params=_cparams("arbitrary", "arbitrary"),
    )(pd, pd, pd, pd, dec, gn, cos, sa, sb)


def _residual_epilogue(y, x_ref, mc_ref, mb_ref, g_ref, is_ctx, x_out_ref, h_out_ref):
    x_new = x_ref[...] + _mod_row(is_ctx, mc_ref, mb_ref, 0) * _rms(y, g_ref[0:1, :])
    x_out_ref[...] = x_new
    h = _rms(x_new, g_ref[1:2, :]) * (1.0 + _mod_row(is_ctx, mc_ref, mb_ref, 2)) + _mod_row(is_ctx, mc_ref, mb_ref, 1)
    h_out_ref[...] = h.astype(BF16)


def _out_proj_kernel(ya_ref, yb_ref, yc_ref, yd_ref, w_ref, x_ref, mc_ref, mb_ref, g_ref, x_out_ref, h_out_ref,
                     *, tm, ctx_len):
    gw = GROUP_W
    y = (jnp.dot(ya_ref[...], w_ref[0:gw, :], preferred_element_type=F32)
         + jnp.dot(yb_ref[...], w_ref[gw:2 * gw, :], preferred_element_type=F32)
         + jnp.dot(yc_ref[...], w_ref[2 * gw:3 * gw, :], preferred_element_type=F32)
         + jnp.dot(yd_ref[...], w_ref[3 * gw:, :], preferred_element_type=F32))
    is_ctx = _ctx_rows(pl.program_id(1), tm, ctx_len)
    _residual_epilogue(y, x_ref, mc_ref, mb_ref, g_ref, is_ctx, x_out_ref, h_out_ref)


def _out_proj(ya, yb, yc, yd, w, x, mc3, mb3, g2, *, nb, tb, ctx_len):
    d = x.shape[1]
    gw = GROUP_W
    tm = _pick_tile(tb, 384)
    nt = tb // tm
    row = lambda b, t: (b * nt + t, 0)
    c2 = lambda b, t: (0, 0)
    return pl.pallas_call(
        functools.partial(_out_proj_kernel, tm=tm, ctx_len=ctx_len),
        grid=(nb, nt),
        in_specs=[pl.BlockSpec((tm, gw), row)] * 4
        + [pl.BlockSpec((4 * gw, d), c2), pl.BlockSpec((tm, d), row),
           pl.BlockSpec((3, d), c2), pl.BlockSpec((1, 3, d), lambda b, t: (b, 0, 0)), pl.BlockSpec((2, d), c2)],
        out_specs=[pl.BlockSpec((tm, d), row), pl.BlockSpec((tm, d), row)],
        out_shape=[jax.ShapeDtypeStruct(x.shape, F32), jax.ShapeDtypeStruct(x.shape, BF16)],
        compiler_params=_cparams("arbitrary", "arbitrary"),
    )(ya, yb, yc, yd, w, x, mc3, mb3, g2)


def _ffn_up_kernel(h_ref, wg_ref, wv_ref, cg_ref, cv_ref, o_ref, *, ctx_len):
    h = h_ref[...]
    starts, ends = _seq_edges(h.shape[0], ctx_len)
    gate = _dwconv3_rows(jnp.dot(h, wg_ref[...], preferred_element_type=F32), cg_ref, starts, ends)
    val = _dwconv3_rows(jnp.dot(h, wv_ref[...], preferred_element_type=F32), cv_ref, starts, ends)
    o_ref[...] = (_silu(gate) * val).astype(o_ref.dtype)


def _ffn_up(h, w_up, w_conv, *, nb, tb, ctx_len):
    r_tot, d = h.shape
    dff = w_up.shape[1] // 2
    tn = 256
    nj = dff // tn
    return pl.pallas_call(
        functools.partial(_ffn_up_kernel, ctx_len=ctx_len),
        grid=(nb, nj),
        in_specs=[pl.BlockSpec((tb, d), lambda b, j: (b, 0)),
                  pl.BlockSpec((d, tn), lambda b, j: (0, j)),
                  pl.BlockSpec((d, tn), lambda b, j: (0, nj + j)),
                  pl.BlockSpec((3, tn), lambda b, j: (0, j)),
                  pl.BlockSpec((3, tn), lambda b, j: (0, nj + j))],
        out_specs=pl.BlockSpec((tb, tn), lambda b, j: (b, j)),
        out_shape=jax.ShapeDtypeStruct((r_tot, dff), BF16),
        compiler_params=_cparams("arbitrary", "arbitrary"),
    )(h, w_up, w_up, w_conv, w_conv)


def _ffn_down_kernel(a_ref, w_ref, x_ref, mc_ref, mb_ref, g_ref, x_out_ref, h_out_ref, acc_ref, *, tm, ctx_len, nk):
    kk = pl.program_id(2)

    @pl.when(kk == 0)
    def _():
        acc_ref[...] = jnp.zeros_like(acc_ref)

    acc_ref[...] += jnp.dot(a_ref[...], w_ref[...], preferred_element_type=F32)

    @pl.when(kk == nk - 1)
    def _():
        is_ctx = _ctx_rows(pl.program_id(1), tm, ctx_len)
        _residual_epilogue(acc_ref[...], x_ref, mc_ref, mb_ref, g_ref, is_ctx, x_out_ref, h_out_ref)


def _ffn_down(act, w, x, mc3, mb3, g2, *, nb, tb, ctx_len):
    d = x.shape[1]
    dff = act.shape[1]
    tm = _pick_tile(tb, 576)
    nt = tb // tm
    tk = 512
    nk = dff // tk
    row = lambda b, t, k: (b * nt + t, 0)
    c2 = lambda b, t, k: (0, 0)
    return pl.pallas_call(
        functools.partial(_ffn_down_kernel, tm=tm, ctx_len=ctx_len, nk=nk),
        grid=(nb, nt, nk),
        in_specs=[pl.BlockSpec((tm, tk), lambda b, t, k: (b * nt + t, k)),
                  pl.BlockSpec((tk, d), lambda b, t, k: (k, 0)),
                  pl.BlockSpec((tm, d), row),
                  pl.BlockSpec((3, d), c2), pl.BlockSpec((1, 3, d), lambda b, t, k: (b, 0, 0)), pl.BlockSpec((2, d), c2)],
        out_specs=[pl.BlockSpec((tm, d), row), pl.BlockSpec((tm, d), row)],
        out_shape=[jax.ShapeDtypeStruct(x.shape, F32), jax.ShapeDtypeStruct(x.shape, BF16)],
        scratch_shapes=[pltpu.VMEM((tm, d), F32)],
        compiler_params=_cparams("arbitrary", "arbitrary", "arbitrary"),
    )(act, w, x, mc3, mb3, g2)


def _pad_cols(w, n):
    return jnp.pad(w, ((0, 0), (0, n - w.shape[1])))


def _layout_w_in(w_in):
    gw = GROUP_W
    a = w_in[:, :RWKV_IN]
    o = 3 * gw
    wa = jnp.concatenate([a[:, :o],
                          _pad_cols(a[:, o:o + RWKV_DECAY_LORA], RWKV_LORA_PAD),
                          _pad_cols(a[:, o + RWKV_DECAY_LORA:o + RWKV_DECAY_LORA + RWKV_ICLR_LORA], RWKV_LORA_PAD),
                          a[:, o + RWKV_DECAY_LORA + RWKV_ICLR_LORA:]], axis=1)
    wb = _pad_cols(w_in[:, RWKV_IN:RWKV_IN + MLA_IN], MLA_IN_PAD)
    wc = w_in[:, RWKV_IN + MLA_IN:RWKV_IN + MLA_IN + CONV_IN]
    wd = w_in[:, RWKV_IN + MLA_IN + CONV_IN:]
    return wa.astype(BF16), wb.astype(BF16), wc.astype(BF16), wd.astype(BF16)


def _layout_rwkv_shift(shift):
    o = 3 * GROUP_W
    return jnp.concatenate([shift[:, :o],
                            _pad_cols(shift[:, o:o + RWKV_DECAY_LORA], RWKV_LORA_PAD),
                            _pad_cols(shift[:, o + RWKV_DECAY_LORA:o + RWKV_DECAY_LORA + RWKV_ICLR_LORA], RWKV_LORA_PAD),
                            shift[:, o + RWKV_DECAY_LORA + RWKV_ICLR_LORA:]], axis=1)


def _pad_lora_rows(w):
    return jnp.pad(w, ((0, 0), (0, RWKV_LORA_PAD - w.shape[1]), (0, 0))).astype(BF16)


def _layout_mla(w_uq, w_ukv):
    hp = MLA_HEAD_PAD
    dqk = MLA_NOPE + MLA_ROPE
    wq = jnp.concatenate([_pad_cols(w_uq[:, h * dqk:(h + 1) * dqk], hp) for h in range(MLA_HEADS)], axis=1)
    dkv = MLA_NOPE + MLA_V
    wk = jnp.concatenate([_pad_cols(w_ukv[:, h * dkv:h * dkv + MLA_NOPE], hp) for h in range(MLA_HEADS)], axis=1)
    wv = jnp.concatenate([w_ukv[:, h * dkv + MLA_NOPE:(h + 1) * dkv] for h in range(MLA_HEADS)], axis=1)
    return wq.astype(BF16), wk.astype(BF16), wv.astype(BF16)


def _rope_tables(seq_len, ctx_len, rot_dim, lead, width):
    rows = seq_len // GRID_W
    half = rot_dim // 2
    quarter = half // 2
    inv = ROPE_BASE ** (-jnp.arange(0, half, 2, dtype=F32) / half)
    row = jnp.repeat(jnp.arange(rows, dtype=F32), GRID_W)
    col = jnp.tile(jnp.arange(GRID_W, dtype=F32), rows)
    zeros = jnp.zeros((seq_len, quarter), F32)
    cos_parts, sa_parts, sb_parts = [], [], []
    for pos in (row, col):
        ang = pos[:, None] * inv[None, :]
        c, s = jnp.cos(ang), jnp.sin(ang)
        cos_parts += [c, c]
        sa_parts += [-s, zeros]
        sb_parts += [zeros, s]

    def table(parts, fill):
        body = jnp.concatenate(parts, axis=1)
        body = jnp.concatenate([jnp.full((seq_len, lead), fill, F32), body,
                                jnp.full((seq_len, width - lead - rot_dim), fill, F32)], axis=1)
        return jnp.concatenate([jnp.full((ctx_len, width), fill, F32), body], axis=0)

    return table(cos_parts, 1.0), table(sa_parts, 0.0), table(sb_parts, 0.0)


def _head_sum_matrix(width, head):
    i = jnp.arange(width) // head
    return (i[:, None] == i[None, :]).astype(BF16)


def _chunk_tri(tm, chunk):
    i = jnp.arange(tm)
    same = (i[:, None] // chunk) == (i[None, :] // chunk)
    lower = jnp.logical_and(same, i[None, :] <= i[:, None])
    upper = jnp.logical_and(same, i[None, :] >= i[:, None])
    return jnp.stack([lower, upper]).astype(BF16)


def _token_mixer(h, lw, tables, *, nb, tb, ctx_len):
    kw = dict(nb=nb, tb=tb)
    wa, wb, wc, wd = lw["w_in"]
    pa = _matmul(h, wa, F32, tn=512, **kw)
    pb = _matmul(h, wb, BF16, tn=MLA_IN_PAD, **kw)
    pc = _matmul(h, wc, BF16, tn=512, **kw)
    pd = _matmul(h, wd, BF16, tn=512, **kw)

    at, rt, bt, kt, wcum, v, gate, bonus = _rwkv_features(
        pa, lw["rwkv_shift"], lw["rwkv_w0"], lw["rwkv_w_up"], lw["rwkv_a0"], lw["rwkv_a_up"], lw["rwkv_g_up"],
        lw["rwkv_vecs"], tables["ones_bd"], tables["tri"], ctx_len=ctx_len, **kw)
    yf, yb_ = _rwkv_scan(at, rt, bt, kt, wcum, v, tables["scan_masks"], tables["same_head"],
                         ctx_len=ctx_len, feat_tm=256, **kw)
    ya = _rwkv_finish(yf, yb_, gate, bonus, lw["rwkv_vecs"], tables["ones_bd"], **kw)

    wq, wk, wv = lw["mla_w"]
    q, k, vv = _mla_project(pb, lw["mla_q_norm"], lw["mla_kv_norm"], wq, wk, wv, *tables["mla_rope"], **kw)
    yb = _mla_attention(q, k, vv, ctx_len=ctx_len, **kw)

    yc = _conv_mixer(pc, lw["conv_w"], ctx_len=ctx_len, **kw)
    yd = _retention(pd, lw["ret_decay"], lw["ret_gn_g"], *tables["ret_rope"], ctx_len=ctx_len, **kw)
    return ya, yb, yc, yd


def kernel(x, c, ctx, c_ctx, mod_w, mod_b, norm_g, w_in, rwkv_shift, rwkv_w0, rwkv_w_up, rwkv_a0, rwkv_a_up,
           rwkv_g_up, rwkv_vecs, mla_q_norm, mla_kv_norm, mla_w_uq, mla_w_ukv, conv_w, ret_decay, ret_gn_g, w_out,
           mlp_w_up, mlp_conv, mlp_w_down):
    nb, seq_len, d = x.shape
    ctx_len = ctx.shape[1]
    depth = mod_w.shape[0]
    tb = ctx_len + seq_len
    assert nb + 1 <= 8 and d == D_MODEL
    kw = dict(nb=nb, tb=tb, ctx_len=ctx_len)

    tables = {
        "ones_bd": _head_sum_matrix(GROUP_W, RWKV_HEAD),
        "tri": _chunk_tri(256, RWKV_CHUNK),
        "scan_masks": _rwkv_scan_masks(),
        "same_head": _head_sum_matrix(RWKV_QUAD, RWKV_HEAD),
        "mla_rope": _rope_tables(seq_len, ctx_len, MLA_ROPE, MLA_NOPE, MLA_HEAD_PAD),
        "ret_rope": _rope_tables(seq_len, ctx_len, RET_HEAD, 0, RET_HEAD),
    }

    c_pad = jnp.concatenate([c, c_ctx[None, :], jnp.zeros((8 - nb - 1, d), F32)], axis=0)
    mods = _modulation(c_pad, mod_w, mod_b).reshape(depth, 8, N_MOD, d)
    m_lat = mods[:, :nb]
    m_ctx = mods[:, nb]

    xs = jnp.concatenate([ctx, x], axis=1).reshape(nb * tb, d)
    h = _prologue(xs, m_ctx[0], m_lat[0], norm_g[0, 0:1], **kw)

    for l in range(depth):
        lw = {
            "w_in": _layout_w_in(w_in[l]),
            "rwkv_shift": _layout_rwkv_shift(rwkv_shift[l]),
            "rwkv_w0": rwkv_w0[l], "rwkv_w_up": _pad_lora_rows(rwkv_w_up[l]),
            "rwkv_a0": rwkv_a0[l], "rwkv_a_up": _pad_lora_rows(rwkv_a_up[l]),
            "rwkv_g_up": rwkv_g_up[l].astype(BF16), "rwkv_vecs": rwkv_vecs[l],
            "mla_q_norm": mla_q_norm[l][None, :], "mla_kv_norm": mla_kv_norm[l][None, :],
            "mla_w": _layout_mla(mla_w_uq[l], mla_w_ukv[l]),
            "conv_w": conv_w[l],
            "ret_decay": jnp.broadcast_to(
                jnp.pad(ret_decay[l].T, ((0, 0), (0, 6)))[:, :, None], (RET_HEADS, 8, 128)),
            "ret_gn_g": ret_gn_g[l][None, :],
        }
        ya, yb, yc, yd = _token_mixer(h, lw, tables, **kw)

        xs, h = _out_proj(ya, yb, yc, yd, w_out[l].astype(BF16), xs,
                          m_ctx[l][jnp.array([2, 3, 4])], m_lat[l][:, jnp.array([2, 3, 4])],
                          norm_g[l, 1:3], **kw)
        act = _ffn_up(h, mlp_w_up[l].astype(BF16), mlp_conv[l], **kw)
        nl = min(l + 1, depth - 1)
        mc3 = jnp.stack([m_ctx[l][5], m_ctx[nl][0], m_ctx[nl][1]])
        mb3 = jnp.stack([m_lat[l][:, 5], m_lat[nl][:, 0], m_lat[nl][:, 1]], axis=1)
        g2 = jnp.stack([norm_g[l, 3], norm_g[nl, 0]])
        xs, h = _ffn_down(act, mlp_w_down[l].astype(BF16), xs, mc3, mb3, g2, **kw)

    return xs.reshape(nb, tb, d)[:, ctx_len:]
```

```python
import functools
import math

import jax
import jax.numpy as jnp
from jax import lax
from jax.experimental import pallas as pl
from jax.experimental.pallas import tpu as pltpu

F32 = jnp.float32
BF16 = jnp.bfloat16

D_MODEL = 2048
GRID_W = 64
GROUP_W = 512
N_MOD = 6
NORM_EPS = 1e-6
ROPE_BASE = 10000.0

RWKV_HEAD = 64
RWKV_HEADS = GROUP_W // RWKV_HEAD
RWKV_DECAY_LORA = 96
RWKV_ICLR_LORA = 96
RWKV_GATE_LORA = 256
RWKV_IN = 3 * GROUP_W + RWKV_DECAY_LORA + RWKV_ICLR_LORA + RWKV_GATE_LORA
RWKV_LORA_PAD = 128
RWKV_IN_PAD = 3 * GROUP_W + 2 * RWKV_LORA_PAD + RWKV_GATE_LORA
RWKV_DECAY_SCALE = math.exp(-0.5)
RWKV_GN_EPS = 64e-5
RWKV_CHUNK = 64

MLA_HEADS = 4
MLA_NOPE = 128
MLA_ROPE = 64
MLA_V = 128
MLA_Q_RANK = 384
MLA_KV_RANK = 256
MLA_IN = MLA_Q_RANK + MLA_KV_RANK + MLA_ROPE
MLA_IN_PAD = 768
MLA_HEAD_PAD = 256
MLA_SCALE = (MLA_NOPE + MLA_ROPE) ** -0.5

CONV_IN = 3 * GROUP_W

RET_HEADS = 4
RET_HEAD = 128
RET_IN = 4 * GROUP_W
GN_EPS = 1e-5

D_FF = 5632
VMEM_LIMIT = 56 * 1024 * 1024


def _cparams(*sem):
    return pltpu.CompilerParams(dimension_semantics=sem, vmem_limit_bytes=VMEM_LIMIT)


def _pick_tile(n, target, mult=16):
    best = None
    for t in range(mult, min(n, target) + 1, mult):
        if n % t == 0:
            best = t
    assert best is not None, (n, target)
    return best


def _dot(a, b):
    return jnp.dot(a.astype(BF16), b.astype(BF16), preferred_element_type=F32)


def _dot_nt(a, b):
    return lax.dot_general(a.astype(BF16), b.astype(BF16), (((1,), (1,)), ((), ())),
                           preferred_element_type=F32)


def _dot_tn(a, b):
    return lax.dot_general(a.astype(BF16), b.astype(BF16), (((0,), (0,)), ((), ())),
                           preferred_element_type=F32)


def _split3(x):
    hi = x.astype(BF16)
    r1 = x - hi.astype(F32)
    mid = r1.astype(BF16)
    lo = (r1 - mid.astype(F32)).astype(BF16)
    return hi, mid, lo


def _dot_wide_rhs(m, x):
    hi, mid, lo = _split3(x)
    return (jnp.dot(m, hi, preferred_element_type=F32) + jnp.dot(m, mid, preferred_element_type=F32)
            + jnp.dot(m, lo, preferred_element_type=F32))


def _dot_wide_lhs(x, m):
    hi, mid, lo = _split3(x)
    return (jnp.dot(hi, m, preferred_element_type=F32) + jnp.dot(mid, m, preferred_element_type=F32)
            + jnp.dot(lo, m, preferred_element_type=F32))


def _sigmoid(x):
    return 1.0 / (1.0 + jnp.exp(-x))


def _silu(x):
    return x * _sigmoid(x)


def _rms(x, g):
    ms = jnp.mean(x * x, axis=-1, keepdims=True)
    return x * lax.rsqrt(ms + NORM_EPS) * g


def _ctx_rows(tile_idx, tm, ctx_len):
    rows = tile_idx * tm + lax.broadcasted_iota(jnp.int32, (tm, 1), 0)
    return rows < ctx_len


def _mod_row(is_ctx, mc_ref, mb_ref, k):
    return jnp.where(is_ctx, mc_ref[k:k + 1, :], mb_ref[0, k:k + 1, :])


def _shift_rows(u, first_row, last_row):
    n = u.shape[0]
    rows = lax.broadcasted_iota(jnp.int32, (n, 1), 0)
    prev = jnp.where(rows == 0, first_row, pltpu.roll(u, 1, 0))
    nxt = jnp.where(rows == n - 1, last_row, pltpu.roll(u, n - 1, 0))
    return prev, nxt


def _rope(x, cos, sa, sb, half):
    n = x.shape[-1]
    return x * cos + pltpu.roll(x, n - half, 1) * sa + pltpu.roll(x, half, 1) * sb


def _mod_kernel(c_ref, w_ref, b_ref, o_ref):
    o_ref[0] = _dot(_silu(c_ref[...]), w_ref[0]) + b_ref[0]


def _modulation(c_pad, mod_w, mod_b):
    depth, d, n = mod_w.shape
    tn = 1024
    return pl.pallas_call(
        _mod_kernel,
        grid=(depth, n // tn),
        in_specs=[pl.BlockSpec((8, d), lambda l, j: (0, 0)),
                  pl.BlockSpec((1, d, tn), lambda l, j: (l, 0, j)),
                  pl.BlockSpec((1, 1, tn), lambda l, j: (l, 0, j))],
        out_specs=pl.BlockSpec((1, 8, tn), lambda l, j: (l, 0, j)),
        out_shape=jax.ShapeDtypeStruct((depth, 8, n), F32),
        compiler_params=_cparams("arbitrary", "arbitrary"),
    )(c_pad, mod_w, mod_b.reshape(depth, 1, n))


def _prologue_kernel(x_ref, mc_ref, mb_ref, g_ref, h_ref, *, tm, ctx_len):
    is_ctx = _ctx_rows(pl.program_id(1), tm, ctx_len)
    h = _rms(x_ref[...], g_ref[...]) * (1.0 + _mod_row(is_ctx, mc_ref, mb_ref, 1)) + _mod_row(is_ctx, mc_ref, mb_ref, 0)
    h_ref[...] = h.astype(BF16)


def _prologue(x, mc, mb, g, *, nb, tb, ctx_len):
    d = x.shape[1]
    tm = _pick_tile(tb, 768)
    nt = tb // tm
    return pl.pallas_call(
        functools.partial(_prologue_kernel, tm=tm, ctx_len=ctx_len),
        grid=(nb, nt),
        in_specs=[pl.BlockSpec((tm, d), lambda b, t: (b * nt + t, 0)),
                  pl.BlockSpec((N_MOD, d), lambda b, t: (0, 0)),
                  pl.BlockSpec((1, N_MOD, d), lambda b, t: (b, 0, 0)),
                  pl.BlockSpec((1, d), lambda b, t: (0, 0))],
        out_specs=pl.BlockSpec((tm, d), lambda b, t: (b * nt + t, 0)),
        out_shape=jax.ShapeDtypeStruct(x.shape, BF16),
        compiler_params=_cparams("arbitrary", "arbitrary"),
    )(x, mc, mb, g)


def _matmul_kernel(a_ref, w_ref, o_ref):
    o_ref[...] = jnp.dot(a_ref[...], w_ref[...], preferred_element_type=F32).astype(o_ref.dtype)


def _matmul(a, w, out_dtype, *, nb, tb, tn):
    k, n = w.shape
    assert n % tn == 0
    return pl.pallas_call(
        _matmul_kernel,
        grid=(nb, n // tn),
        in_specs=[pl.BlockSpec((tb, k), lambda b, j: (b, 0)),
                  pl.BlockSpec((k, tn), lambda b, j: (0, j))],
        out_specs=pl.BlockSpec((tb, tn), lambda b, j: (b, j)),
        out_shape=jax.ShapeDtypeStruct((a.shape[0], n), out_dtype),
        compiler_params=_cparams("arbitrary", "arbitrary"),
    )(a, w)


def _rwkv_feat_kernel(p_ref, pprev_ref, pnext_ref, shift_ref, w0_ref, wup_ref, a0_ref, aup_ref, gup_ref,
                      vecs_ref, ones_ref, tri_ref,
                      at_ref, rt_ref, bt_ref, kt_ref, wc_ref, v_ref, gate_ref, bonus_ref,
                      *, tm, ctx_tiles, nt):
    t = pl.program_id(1)
    first = jnp.logical_or(t == 0, t == ctx_tiles)
    last = jnp.logical_or(t == ctx_tiles - 1, t == nt - 1)
    u = p_ref[...]
    halo_prev = jnp.where(first, 0.0, pprev_ref[7:8, :])
    halo_next = jnp.where(last, 0.0, pnext_ref[0:1, :])
    prev, nxt = _shift_rows(u, halo_prev, halo_next)
    p = u + shift_ref[0:1, :] * (prev - u) + shift_ref[1:2, :] * (nxt - u)

    gw = GROUP_W
    r = p[:, 0:gw]
    k = p[:, gw:2 * gw]
    v = p[:, 2 * gw:3 * gw]
    wd = p[:, 3 * gw:3 * gw + RWKV_LORA_PAD]
    ad = p[:, 3 * gw + RWKV_LORA_PAD:3 * gw + 2 * RWKV_LORA_PAD]
    gd = p[:, 3 * gw + 2 * RWKV_LORA_PAD:]
    k_k = vecs_ref[0:1, :]
    k_a = vecs_ref[1:2, :]
    r_k = vecs_ref[2:3, :]
    ones = ones_ref[...]

    kk = k * k_k
    kk = kk * lax.rsqrt(_dot_wide_lhs(kk * kk, ones) + 1e-12)
    v_ref[...] = v.astype(BF16)
    gate_ref[...] = _dot(_sigmoid(gd), gup_ref[...])
    bonus_ref[...] = _dot_wide_lhs(r * k * r_k, ones) * v

    tanh_wd = jnp.tanh(wd)
    nch = tm // RWKV_CHUNK
    for d in range(2):
        lw = -RWKV_DECAY_SCALE * _sigmoid(w0_ref[d:d + 1, :] + _dot(tanh_wd, wup_ref[d]))
        l_inc = _dot_wide_rhs(tri_ref[d], lw)
        l_exc = l_inc - lw
        asig = _sigmoid(a0_ref[d:d + 1, :] + _dot(ad, aup_ref[d]))
        k_d = k * (1.0 + (asig - 1.0) * k_a)
        e_inc = jnp.exp(l_inc)
        e_neg = jnp.exp(-l_inc)
        at_ref[d] = (-kk * jnp.exp(l_exc)).astype(BF16)
        rt_ref[d] = (r * e_inc).astype(BF16)
        bt_ref[d] = (kk * asig * e_neg).astype(BF16)
        kt_ref[d] = (k_d * e_neg).astype(BF16)
        end = RWKV_CHUNK - 1 if d == 0 else 0
        rows = [e_inc[c * RWKV_CHUNK + end:c * RWKV_CHUNK + end + 1, :] for c in range(nch)]
        rows.append(jnp.zeros((8 - nch, gw), F32))
        wc_ref[d] = jnp.concatenate(rows, axis=0)


def _rwkv_features(pa, shift, w0, wup, a0, aup, gup, vecs, ones_bd, tri, *, nb, tb, ctx_len):
    r_tot = pa.shape[0]
    tm = 256
    assert ctx_len % tm == 0 and tb % tm == 0
    nt = tb // tm
    hb = tm // 8
    nblk8 = r_tot // 8
    gw = GROUP_W
    row = lambda b, t: (b * nt + t, 0)
    drow = lambda b, t: (0, b * nt + t, 0)
    const2 = lambda b, t: (0, 0)
    const3 = lambda b, t: (0, 0, 0)
    feat = jax.ShapeDtypeStruct((2, r_tot, gw), BF16)
    return pl.pallas_call(
        functools.partial(_rwkv_feat_kernel, tm=tm, ctx_tiles=ctx_len // tm, nt=nt),
        grid=(nb, nt),
        in_specs=[pl.BlockSpec((tm, RWKV_IN_PAD), row),
                  pl.BlockSpec((8, RWKV_IN_PAD), lambda b, t: (jnp.maximum((b * nt + t) * hb - 1, 0), 0)),
                  pl.BlockSpec((8, RWKV_IN_PAD), lambda b, t: (jnp.minimum((b * nt + t + 1) * hb, nblk8 - 1), 0)),
                  pl.BlockSpec((2, RWKV_IN_PAD), const2),
                  pl.BlockSpec((2, gw), const2),
                  pl.BlockSpec((2, RWKV_LORA_PAD, gw), const3),
                  pl.BlockSpec((2, gw), const2),
                  pl.BlockSpec((2, RWKV_LORA_PAD, gw), const3),
                  pl.BlockSpec((RWKV_GATE_LORA, gw), const2),
                  pl.BlockSpec((5, gw), const2),
                  pl.BlockSpec((gw, gw), const2),
                  pl.BlockSpec((2, tm, tm), const3)],
        out_specs=[pl.BlockSpec((2, tm, gw), drow)] * 4
        + [pl.BlockSpec((2, 8, gw), drow),
           pl.BlockSpec((tm, gw), row), pl.BlockSpec((tm, gw), row), pl.BlockSpec((tm, gw), row)],
        out_shape=[feat, feat, feat, feat,
                   jax.ShapeDtypeStruct((2, r_tot // tm * 8, gw), F32),
                   jax.ShapeDtypeStruct((r_tot, gw), BF16),
                   jax.ShapeDtypeStruct((r_tot, gw), F32),
                   jax.ShapeDtypeStruct((r_tot, gw), F32)],
        compiler_params=_cparams("arbitrary", "arbitrary"),
    )(pa, pa, pa, shift, w0, wup, a0, aup, gup, vecs, ones_bd, tri)


RWKV_QUAD = 4 * RWKV_HEAD
RWKV_INV_LEVELS = 6
M_STRICT, M_INCL, M_LEVEL0, M_EYE, M_SAME = 0, 1, 2, 2 + RWKV_INV_LEVELS, 3 + RWKV_INV_LEVELS
RWKV_N_MASKS = 4 + RWKV_INV_LEVELS


def _rwkv_scan_masks():
    n, c = RWKV_QUAD, RWKV_CHUNK
    r = jnp.arange(n)[:, None]
    col = jnp.arange(n)[None, :]
    same = (r // c) == (col // c)
    t, j = r % c, col % c
    out = []
    for d in range(2):
        before = (j < t) if d == 0 else (j > t)
        ms = [same & before, same & (before | (j == t))]
        for lvl in range(RWKV_INV_LEVELS):
            s = 2 ** lvl
            blk = (r // (2 * s)) == (col // (2 * s))
            late_r = (r % (2 * s)) >= s
            late_c = (col % (2 * s)) >= s
            ms.append(blk & ((late_r & ~late_c) if d == 0 else (~late_r & late_c)))
        ms += [r == col, same]
        out.append(jnp.stack(ms))
    return jnp.stack(out).astype(F32)


def _head_stack(x):
    hd = RWKV_HEAD
    return jnp.concatenate([x[:, h * hd:(h + 1) * hd] for h in range(4)], axis=0)


def _head_unstack(x):
    c = RWKV_CHUNK
    return jnp.concatenate([x[h * c:(h + 1) * c, :] for h in range(4)], axis=1)


def _rwkv_scan_kernel(atf_ref, rtf_ref, btf_ref, ktf_ref, vf_ref, wcf_ref,
                      atb_ref, rtb_ref, btb_ref, ktb_ref, vb_ref, wcb_ref, mask_ref, same_ref,
                      yf_ref, yb_ref, s_ref, *, n_chunks, ctx_chunks, per_tile):
    i = pl.program_id(1)

    @pl.when(i == 0)
    def _():
        s_ref[...] = jnp.zeros_like(s_ref)

    c_bwd = jnp.where(i < ctx_chunks, ctx_chunks - 1 - i, n_chunks - 1 + ctx_chunks - i)
    wrow = (i % per_tile, c_bwd % per_tile)
    ins = ((atf_ref, rtf_ref, btf_ref, ktf_ref, vf_ref, wcf_ref, yf_ref),
           (atb_ref, rtb_ref, btb_ref, ktb_ref, vb_ref, wcb_ref, yb_ref))
    qw = RWKV_QUAD
    n4 = 4 * RWKV_CHUNK
    chains = [(d, q) for d in range(2) for q in range(GROUP_W // qw)]

    def masked_stack(x):
        return jnp.concatenate([x, x, x, x], axis=0) * same_ref[...]

    def block_diag(x_st, d, m):
        return jnp.concatenate([x_st, x_st, x_st, x_st], axis=1) * mask_ref[d, m]

    st = {}
    for ch in chains:
        d, q = ch
        a_ref, r_ref, b_ref, k_ref, v_ref, _, _ = ins[d]
        sl = slice(q * qw, (q + 1) * qw)
        b_, k_ = b_ref[0, :, sl], k_ref[0, :, sl]
        st[ch] = dict(ms_ar=jnp.concatenate([masked_stack(a_ref[0, :, sl]), masked_stack(r_ref[0, :, sl])], axis=0),
                      ms_b=masked_stack(b_), ms_k=masked_stack(k_), b=b_, k=k_, v_st=_head_stack(v_ref[:, sl]))
    for ch in chains:
        e = st[ch]
        e["aa_b"] = _dot_nt(e["ms_ar"], e["b"])
        e["aa_k"] = _dot_nt(e["ms_ar"], e["k"])
    for ch in chains:
        d = ch[0]
        e = st[ch]
        e["a_ab"] = block_diag(e["aa_b"][:n4], d, M_STRICT)
        e["a_ak"] = block_diag(e["aa_k"][:n4], d, M_STRICT)
        e["a_rb"] = block_diag(e["aa_b"][n4:], d, M_INCL)
        e["a_rk"] = block_diag(e["aa_k"][n4:], d, M_INCL)
        e["t"] = mask_ref[d, M_EYE] + e["a_ab"] * mask_ref[d, M_LEVEL0]
    for lvl in range(1, RWKV_INV_LEVELS):
        for ch in chains:
            e = st[ch]
            e["w"] = _dot(e["a_ab"] * mask_ref[ch[0], M_LEVEL0 + lvl], e["t"])
        for ch in chains:
            e = st[ch]
            e["t"] = e["t"] + _dot(e["t"], e["w"])
    for ch in chains:
        e = st[ch]
        e["akv"] = _dot(e["a_ak"], e["v_st"])
        e["rkv"] = _dot(e["a_rk"], e["v_st"])
        e["vtk"] = _dot_tn(e["v_st"], e["ms_k"])
    for ch in chains:
        d, q = ch
        e = st[ch]
        e["s"] = s_ref[d, :, q * qw:(q + 1) * qw]
        e["xs"] = _dot_nt(e["ms_ar"], e["s"])
    for ch in chains:
        e = st[ch]
        e["u"] = _dot(e["t"], e["xs"][:n4] + e["akv"])
    for ch in chains:
        d, q = ch
        e = st[ch]
        sl = slice(q * qw, (q + 1) * qw)
        y_st = e["xs"][n4:] + _dot(e["a_rb"], e["u"]) + e["rkv"]
        ins[d][6][:, sl] = _head_unstack(y_st)
        wc = ins[d][5][0, pl.ds(wrow[d], 1), sl]
        s_ref[d, :, sl] = (e["s"] + _dot_tn(e["u"], e["ms_b"]) + e["vtk"]) * wc


def _rwkv_scan(at, rt, bt, kt, wc, v, masks, same, *, nb, tb, ctx_len, feat_tm):
    r_tot = v.shape[0]
    gw = GROUP_W
    c = RWKV_CHUNK
    n_chunks = tb // c
    ctx_chunks = ctx_len // c
    per_tile = feat_tm // c

    def cf(b, i):
        return b * n_chunks + i

    def cb(b, i):
        return b * n_chunks + jnp.where(i < ctx_chunks, ctx_chunks - 1 - i, n_chunks - 1 + ctx_chunks - i)

    def specs(d, cidx):
        feat = pl.BlockSpec((1, c, gw), lambda b, i: (d, cidx(b, i), 0))
        return [feat, feat, feat, feat,
                pl.BlockSpec((c, gw), lambda b, i: (cidx(b, i), 0)),
                pl.BlockSpec((1, 8, gw), lambda b, i: (d, cidx(b, i) // per_tile, 0))]

    return pl.pallas_call(
        functools.partial(_rwkv_scan_kernel, n_chunks=n_chunks, ctx_chunks=ctx_chunks, per_tile=per_tile),
        grid=(nb, n_chunks),
        in_specs=specs(0, cf) + specs(1, cb)
        + [pl.BlockSpec((2, RWKV_N_MASKS, RWKV_QUAD, RWKV_QUAD), lambda b, i: (0, 0, 0, 0)),
           pl.BlockSpec((RWKV_QUAD, RWKV_QUAD), lambda b, i: (0, 0))],
        out_specs=[pl.BlockSpec((c, gw), lambda b, i: (cf(b, i), 0)),
                   pl.BlockSpec((c, gw), lambda b, i: (cb(b, i), 0))],
        out_shape=[jax.ShapeDtypeStruct((r_tot, gw), F32), jax.ShapeDtypeStruct((r_tot, gw), F32)],
        scratch_shapes=[pltpu.VMEM((2, RWKV_HEAD, gw), F32)],
        compiler_params=_cparams("arbitrary", "arbitrary"),
    )(at, rt, bt, kt, v, wc, at, rt, bt, kt, v, wc, masks, same)


def _rwkv_finish_kernel(yf_ref, yb_ref, gate_ref, bonus_ref, vecs_ref, ones_ref, o_ref):
    ones = ones_ref[...]
    y = yf_ref[...] + yb_ref[...]
    inv = 1.0 / RWKV_HEAD
    mu = _dot_wide_lhs(y, ones) * inv
    yc = y - mu
    var = _dot_wide_lhs(yc * yc, ones) * inv
    out = yc * lax.rsqrt(var + RWKV_GN_EPS) * vecs_ref[3:4, :] + vecs_ref[4:5, :] + bonus_ref[...]
    o_ref[...] = (out * gate_ref[...]).astype(o_ref.dtype)


def _rwkv_finish(yf, yb, gate, bonus, vecs, ones_bd, *, nb, tb):
    r_tot = gate.shape[0]
    gw = GROUP_W
    tm = _pick_tile(tb, 768)
    nt = tb // tm
    row = lambda b, t: (b * nt + t, 0)
    return pl.pallas_call(
        _rwkv_finish_kernel,
        grid=(nb, nt),
        in_specs=[pl.BlockSpec((tm, gw), row), pl.BlockSpec((tm, gw), row),
                  pl.BlockSpec((tm, gw), row), pl.BlockSpec((tm, gw), row),
                  pl.BlockSpec((5, gw), lambda b, t: (0, 0)),
                  pl.BlockSpec((gw, gw), lambda b, t: (0, 0))],
        out_specs=pl.BlockSpec((tm, gw), row),
        out_shape=jax.ShapeDtypeStruct((r_tot, gw), BF16),
        compiler_params=_cparams("arbitrary", "arbitrary"),
    )(yf, yb, gate, bonus, vecs, ones_bd)


def _mla_proj_kernel(p_ref, qn_ref, kvn_ref, wq_ref, wk_ref, wv_ref, cos_ref, sa_ref, sb_ref,
                     q_ref, k_ref, v_ref):
    p = p_ref[...].astype(F32)
    c_q = p[:, :MLA_Q_RANK]
    c_kv = p[:, MLA_Q_RANK:MLA_Q_RANK + MLA_KV_RANK]
    k_r = p[:, MLA_Q_RANK + MLA_KV_RANK:]
    n_q = _rms(c_q, qn_ref[...])
    n_kv = _rms(c_kv, kvn_ref[...])
    q = _dot(n_q, wq_ref[...])
    kx = _dot(n_kv, wk_ref[...])
    v_ref[...] = _dot(n_kv, wv_ref[...]).astype(BF16)
    cos, sa, sb = cos_ref[...], sa_ref[...], sb_ref[...]
    kr_blk = jnp.concatenate([jnp.zeros_like(k_r), k_r], axis=1)
    hp = MLA_HEAD_PAD
    for h in range(MLA_HEADS):
        sl = slice(h * hp, (h + 1) * hp)
        q_ref[:, sl] = _rope(q[:, sl], cos, sa, sb, MLA_ROPE // 4).astype(BF16)
        k_ref[:, sl] = _rope(kx[:, sl] + kr_blk, cos, sa, sb, MLA_ROPE // 4).astype(BF16)


def _mla_project(pb, qn, kvn, wq, wk, wv, cos, sa, sb, *, nb, tb):
    r_tot = pb.shape[0]
    tm = _pick_tile(tb, 768)
    nt = tb // tm
    row = lambda b, t: (b * nt + t, 0)
    trow = lambda b, t: (t, 0)
    c2 = lambda b, t: (0, 0)
    hw = MLA_HEADS * MLA_HEAD_PAD
    return pl.pallas_call(
        _mla_proj_kernel,
        grid=(nb, nt),
        in_specs=[pl.BlockSpec((tm, MLA_IN_PAD), row),
                  pl.BlockSpec((1, MLA_Q_RANK), c2), pl.BlockSpec((1, MLA_KV_RANK), c2),
                  pl.BlockSpec((MLA_Q_RANK, hw), c2), pl.BlockSpec((MLA_KV_RANK, hw), c2),
                  pl.BlockSpec((MLA_KV_RANK, GROUP_W), c2),
                  pl.BlockSpec((tm, MLA_HEAD_PAD), trow), pl.BlockSpec((tm, MLA_HEAD_PAD), trow),
                  pl.BlockSpec((tm, MLA_HEAD_PAD), trow)],
        out_specs=[pl.BlockSpec((tm, hw), row), pl.BlockSpec((tm, hw), row), pl.BlockSpec((tm, GROUP_W), row)],
        out_shape=[jax.ShapeDtypeStruct((r_tot, hw), BF16), jax.ShapeDtypeStruct((r_tot, hw), BF16),
                   jax.ShapeDtypeStruct((r_tot, GROUP_W), BF16)],
        compiler_params=_cparams("arbitrary", "arbitrary"),
    )(pb, qn, kvn, wq, wk, wv, cos, sa, sb)


def _softmax_pv(s, v):
    s = s * MLA_SCALE
    m = jnp.max(s, axis=-1, keepdims=True)
    e = jnp.exp(s - m)
    l = jnp.sum(e, axis=-1, keepdims=True)
    return _dot(e, v) / l


def _mla_attn_kernel(q_ref, k_ref, v_ref, o_ref, *, ctx_len, tq, n_qt):
    k_all = k_ref[...]
    v_all = v_ref[...]
    o_ref[0:ctx_len, :] = _softmax_pv(_dot_nt(q_ref[0:ctx_len, :], k_all[0:ctx_len]), v_all[0:ctx_len]).astype(BF16)

    for i in range(n_qt):
        rows = slice(ctx_len + i * tq, ctx_len + (i + 1) * tq)
        o_ref[rows, :] = _softmax_pv(_dot_nt(q_ref[rows, :], k_all), v_all).astype(BF16)


def _mla_attention(q, k, v, *, nb, tb, ctx_len):
    r_tot = q.shape[0]
    tq = 256
    assert (tb - ctx_len) % tq == 0 and ctx_len % tq == 0
    hp = MLA_HEAD_PAD
    return pl.pallas_call(
        functools.partial(_mla_attn_kernel, ctx_len=ctx_len, tq=tq, n_qt=(tb - ctx_len) // tq),
        grid=(nb, MLA_HEADS),
        in_specs=[pl.BlockSpec((tb, hp), lambda b, h: (b, h)),
                  pl.BlockSpec((tb, hp), lambda b, h: (b, h)),
                  pl.BlockSpec((tb, MLA_V), lambda b, h: (b, h))],
        out_specs=pl.BlockSpec((tb, MLA_V), lambda b, h: (b, h)),
        out_shape=jax.ShapeDtypeStruct((r_tot, GROUP_W), BF16),
        compiler_params=_cparams("arbitrary", "arbitrary"),
    )(q, k, v)


def _seq_edges(n, ctx_len):
    rows = lax.broadcasted_iota(jnp.int32, (n, 1), 0)
    starts = jnp.logical_or(rows == 0, rows == ctx_len)
    ends = jnp.logical_or(rows == ctx_len - 1, rows == n - 1)
    return starts, ends


def _dwconv3_rows(u, w_ref, starts, ends):
    n = u.shape[0]
    prev = jnp.where(starts, 0.0, pltpu.roll(u, 1, 0))
    nxt = jnp.where(ends, 0.0, pltpu.roll(u, n - 1, 0))
    return prev * w_ref[0:1, :] + u * w_ref[1:2, :] + nxt * w_ref[2:3, :]


def _conv_mix_kernel(b_ref, c_ref, u_ref, w_ref, o_ref, *, ctx_len):
    starts, ends = _seq_edges(b_ref.shape[0], ctx_len)
    z = c_ref[...].astype(F32) * u_ref[...].astype(F32)
    o_ref[...] = (b_ref[...].astype(F32) * _dwconv3_rows(z, w_ref, starts, ends)).astype(o_ref.dtype)


def _conv_mixer(pc, conv_w, *, nb, tb, ctx_len):
    r_tot = pc.shape[0]
    tc = 256
    nj = GROUP_W // tc
    return pl.pallas_call(
        functools.partial(_conv_mix_kernel, ctx_len=ctx_len),
        grid=(nb, nj),
        in_specs=[pl.BlockSpec((tb, tc), lambda b, j: (b, j)),
                  pl.BlockSpec((tb, tc), lambda b, j: (b, nj + j)),
                  pl.BlockSpec((tb, tc), lambda b, j: (b, 2 * nj + j)),
                  pl.BlockSpec((3, tc), lambda b, j: (0, j))],
        out_specs=pl.BlockSpec((tb, tc), lambda b, j: (b, j)),
        out_shape=jax.ShapeDtypeStruct((r_tot, GROUP_W), BF16),
        compiler_params=_cparams("arbitrary", "arbitrary"),
    )(pc, pc, pc, conv_w)


def _ret_kernel(q_ref, k_ref, v_ref, g_ref, dec_ref, gn_ref, cos_ref, sa_ref, sb_ref, o_ref,
                qs_ref, ks_ref, vs_ref, dm_ref, *, ctx_len, seq_len):
    blk = ctx_len
    nlat = seq_len // blk
    tb = ctx_len + seq_len
    cos, sa, sb = cos_ref[...], sa_ref[...], sb_ref[...]
    qs_ref[...] = _rope(q_ref[...].astype(F32), cos, sa, sb, RET_HEAD // 4).astype(BF16)
    kr = _rope(k_ref[...].astype(F32), cos, sa, sb, RET_HEAD // 4).astype(BF16)
    ks_ref[0:tb, :] = kr
    ks_ref[tb:tb + blk, :] = kr[0:blk]
    vs_ref[0:tb, :] = v_ref[...]
    vs_ref[tb:tb + blk, :] = v_ref[0:blk, :]
    lgf = -jnp.exp(dec_ref[0, 0:1, 0:1])
    lgb = -jnp.exp(dec_ref[0, 1:2, 0:1])
    gn = gn_ref[...]
    scale = RET_HEAD ** -0.5
    rc = (lax.broadcasted_iota(jnp.int32, (blk, blk), 0) - lax.broadcasted_iota(jnp.int32, (blk, blk), 1)).astype(F32)
    dm_ref[0] = jnp.exp(jnp.where(rc >= 0, rc * lgf, -rc * lgb))
    for dist in range(1, nlat + 1):
        dm_ref[dist] = jnp.exp((dist * blk + rc) * lgf)
        dm_ref[nlat + dist] = jnp.exp((dist * blk - rc) * lgb)

    def finish(o, g):
        mu = jnp.mean(o, axis=-1, keepdims=True)
        oc = o - mu
        var = jnp.mean(oc * oc, axis=-1, keepdims=True)
        return _silu(g) * (oc * lax.rsqrt(var + GN_EPS) * gn)

    s_c = _dot_nt(qs_ref[0:blk, :], ks_ref[0:blk, :]) * dm_ref[0]
    o_c = _dot(s_c, vs_ref[0:blk, :]) * scale
    o_ref[0:blk, :] = finish(o_c, g_ref[0:blk, :].astype(F32)).astype(o_ref.dtype)

    for i in range(nlat):
        rows = slice(ctx_len + i * blk, ctx_len + (i + 1) * blk)
        s = _dot_nt(qs_ref[rows, :], ks_ref[...])
        parts = []
        for jp in range(nlat + 2):
            j = jp - 1
            m = dm_ref[i - j] if j < i else (dm_ref[0] if j == i else dm_ref[nlat + j - i])
            parts.append((s[:, jp * blk:(jp + 1) * blk] * m).astype(BF16))
        o = jnp.dot(jnp.concatenate(parts, axis=1), vs_ref[...], preferred_element_type=F32) * scale
        o_ref[rows, :] = finish(o, g_ref[rows, :].astype(F32)).astype(o_ref.dtype)


def _retention(pd, dec, gn, cos, sa, sb, *, nb, tb, ctx_len):
    r_tot = pd.shape[0]
    hd = RET_HEAD
    nh = RET_HEADS
    seq_len = tb - ctx_len
    assert seq_len % ctx_len == 0
    nlat = seq_len // ctx_len
    tbl = pl.BlockSpec((tb, hd), lambda b, h: (0, 0))
    return pl.pallas_call(
        functools.partial(_ret_kernel, ctx_len=ctx_len, seq_len=seq_len),
        grid=(nb, nh),
        in_specs=[pl.BlockSpec((tb, hd), lambda b, h: (b, h)),
                  pl.BlockSpec((tb, hd), lambda b, h: (b, nh + h)),
                  pl.BlockSpec((tb, hd), lambda b, h: (b, 2 * nh + h)),
                  pl.BlockSpec((tb, hd), lambda b, h: (b, 3 * nh + h)),
                  pl.BlockSpec((1, 8, 128), lambda b, h: (h, 0, 0)),
                  pl.BlockSpec((1, hd), lambda b, h: (0, h)),
                  tbl, tbl, tbl],
        out_specs=pl.BlockSpec((tb, hd), lambda b, h: (b, h)),
        out_shape=jax.ShapeDtypeStruct((r_tot, GROUP_W), BF16),
        scratch_shapes=[pltpu.VMEM((tb, hd), BF16), pltpu.VMEM((tb + ctx_len, hd), BF16),
                        pltpu.VMEM((tb + ctx_len, hd), BF16), pltpu.VMEM((2 * nlat + 1, ctx_len, ctx_len), F32)],
        compiler_params=_cparams("arbitrary", "arbitrary"),
    )(pd, pd, pd, pd, dec, gn, cos, sa, sb)


def _residual_epilogue(y, x_ref, mc_ref, mb_ref, g_ref, is_ctx, x_out_ref, h_out_ref):
    x_new = x_ref[...] + _mod_row(is_ctx, mc_ref, mb_ref, 0) * _rms(y, g_ref[0:1, :])
    x_out_ref[...] = x_new
    h = _rms(x_new, g_ref[1:2, :]) * (1.0 + _mod_row(is_ctx, mc_ref, mb_ref, 2)) + _mod_row(is_ctx, mc_ref, mb_ref, 1)
    h_out_ref[...] = h.astype(BF16)


def _out_proj_kernel(ya_ref, yb_ref, yc_ref, yd_ref, w_ref, x_ref, mc_ref, mb_ref, g_ref, x_out_ref, h_out_ref,
                     *, tm, ctx_len):
    gw = GROUP_W
    y = (jnp.dot(ya_ref[...], w_ref[0:gw, :], preferred_element_type=F32)
         + jnp.dot(yb_ref[...], w_ref[gw:2 * gw, :], preferred_element_type=F32)
         + jnp.dot(yc_ref[...], w_ref[2 * gw:3 * gw, :], preferred_element_type=F32)
         + jnp.dot(yd_ref[...], w_ref[3 * gw:, :], preferred_element_type=F32))
    is_ctx = _ctx_rows(pl.program_id(1), tm, ctx_len)
    _residual_epilogue(y, x_ref, mc_ref, mb_ref, g_ref, is_ctx, x_out_ref, h_out_ref)


def _out_proj(ya, yb, yc, yd, w, x, mc3, mb3, g2, *, nb, tb, ctx_len):
    d = x.shape[1]
    gw = GROUP_W
    tm = _pick_tile(tb, 768)
    nt = tb // tm
    row = lambda b, t: (b * nt + t, 0)
    c2 = lambda b, t: (0, 0)
    return pl.pallas_call(
        functools.partial(_out_proj_kernel, tm=tm, ctx_len=ctx_len),
        grid=(nb, nt),
        in_specs=[pl.BlockSpec((tm, gw), row)] * 4
        + [pl.BlockSpec((4 * gw, d), c2, pipeline_mode=pl.Buffered(1)), pl.BlockSpec((tm, d), row),
           pl.BlockSpec((3, d), c2), pl.BlockSpec((1, 3, d), lambda b, t: (b, 0, 0)), pl.BlockSpec((2, d), c2)],
        out_specs=[pl.BlockSpec((tm, d), row), pl.BlockSpec((tm, d), row)],
        out_shape=[jax.ShapeDtypeStruct(x.shape, F32), jax.ShapeDtypeStruct(x.shape, BF16)],
        compiler_params=_cparams("arbitrary", "arbitrary"),
    )(ya, yb, yc, yd, w, x, mc3, mb3, g2)


FFN_ROW_CHUNK = 768
FFN_PAD = 8


def _ffn_up_kernel(h_ref, wg_ref, wv_ref, cg_ref, cv_ref, o_ref, ug_ref, uv_ref, *, ctx_len):
    n, tn = o_ref.shape
    rc, pad = FFN_ROW_CHUNK, FFN_PAD
    nch = n // rc
    wg = wg_ref[...].astype(BF16)
    wv = wv_ref[...].astype(BF16)
    for ref in (ug_ref, uv_ref):
        ref[0:pad, :] = jnp.zeros((pad, tn), F32)
        ref[pad + n:2 * pad + n, :] = jnp.zeros((pad, tn), F32)
    local = lax.broadcasted_iota(jnp.int32, (rc, 1), 0)

    def matmuls(c):
        h = h_ref[c * rc:(c + 1) * rc, :]
        ug_ref[pad + c * rc:pad + (c + 1) * rc, :] = jnp.dot(h, wg, preferred_element_type=F32)
        uv_ref[pad + c * rc:pad + (c + 1) * rc, :] = jnp.dot(h, wv, preferred_element_type=F32)

    def conv(ref, w_ref, c):
        r0 = c * rc
        prev = ref[pad + r0 - 1:pad + r0 - 1 + rc, :]
        cur = ref[pad + r0:pad + r0 + rc, :]
        nxt = ref[pad + r0 + 1:pad + r0 + 1 + rc, :]
        if r0 <= ctx_len < r0 + rc:
            prev = jnp.where(local == ctx_len - r0, 0.0, prev)
        if r0 <= ctx_len - 1 < r0 + rc:
            nxt = jnp.where(local == ctx_len - 1 - r0, 0.0, nxt)
        return prev * w_ref[0:1, :] + cur * w_ref[1:2, :] + nxt * w_ref[2:3, :]

    def epilogue(c):
        gate = conv(ug_ref, cg_ref, c)
        val = conv(uv_ref, cv_ref, c)
        o_ref[c * rc:(c + 1) * rc, :] = (_silu(gate) * val).astype(o_ref.dtype)

    for c in range(nch + 2):
        if c < nch:
            matmuls(c)
        if c >= 2:
            epilogue(c - 2)


def _ffn_up(h, w_up, w_conv, *, nb, tb, ctx_len):
    r_tot, d = h.shape
    dff = w_up.shape[1] // 2
    tn = 256
    nj = dff // tn
    assert tb % FFN_ROW_CHUNK == 0
    return pl.pallas_call(
        functools.partial(_ffn_up_kernel, ctx_len=ctx_len),
        grid=(nb, nj),
        in_specs=[pl.BlockSpec((tb, d), lambda b, j: (b, 0)),
                  pl.BlockSpec((d, tn), lambda b, j: (0, j)),
                  pl.BlockSpec((d, tn), lambda b, j: (0, nj + j)),
                  pl.BlockSpec((3, tn), lambda b, j: (0, j)),
                  pl.BlockSpec((3, tn), lambda b, j: (0, nj + j))],
        out_specs=pl.BlockSpec((tb, tn), lambda b, j: (b, j)),
        out_shape=jax.ShapeDtypeStruct((r_tot, dff), BF16),
        scratch_shapes=[pltpu.VMEM((tb + 2 * FFN_PAD, tn), F32), pltpu.VMEM((tb + 2 * FFN_PAD, tn), F32)],
        compiler_params=_cparams("arbitrary", "arbitrary"),
    )(h, w_up, w_up, w_conv, w_conv)


def _ffn_down_kernel(a_ref, w_ref, x_ref, mc_ref, mb_ref, g_ref, x_out_ref, h_out_ref, *, tm, ctx_len):
    y = jnp.dot(a_ref[...], w_ref[...], preferred_element_type=F32)
    is_ctx = _ctx_rows(pl.program_id(1), tm, ctx_len)
    _residual_epilogue(y, x_ref, mc_ref, mb_ref, g_ref, is_ctx, x_out_ref, h_out_ref)


def _ffn_down(act, w, x, mc3, mb3, g2, *, nb, tb, ctx_len):
    d = x.shape[1]
    dff = act.shape[1]
    tm = _pick_tile(tb, 384)
    nt = tb // tm
    row = lambda b, t: (b * nt + t, 0)
    c2 = lambda b, t: (0, 0)
    return pl.pallas_call(
        functools.partial(_ffn_down_kernel, tm=tm, ctx_len=ctx_len),
        grid=(nb, nt),
        in_specs=[pl.BlockSpec((tm, dff), row),
                  pl.BlockSpec((dff, d), c2, pipeline_mode=pl.Buffered(1)),
                  pl.BlockSpec((tm, d), row),
                  pl.BlockSpec((3, d), c2), pl.BlockSpec((1, 3, d), lambda b, t: (b, 0, 0)), pl.BlockSpec((2, d), c2)],
        out_specs=[pl.BlockSpec((tm, d), row), pl.BlockSpec((tm, d), row)],
        out_shape=[jax.ShapeDtypeStruct(x.shape, F32), jax.ShapeDtypeStruct(x.shape, BF16)],
        compiler_params=_cparams("arbitrary", "arbitrary"),
    )(act, w, x, mc3, mb3, g2)


def _pad_cols(w, n):
    return jnp.pad(w, ((0, 0), (0, n - w.shape[1])))


def _layout_w_in(w_in):
    gw = GROUP_W
    a = w_in[:, :RWKV_IN]
    o = 3 * gw
    wa = jnp.concatenate([a[:, :o],
                          _pad_cols(a[:, o:o + RWKV_DECAY_LORA], RWKV_LORA_PAD),
                          _pad_cols(a[:, o + RWKV_DECAY_LORA:o + RWKV_DECAY_LORA + RWKV_ICLR_LORA], RWKV_LORA_PAD),
                          a[:, o + RWKV_DECAY_LORA + RWKV_ICLR_LORA:]], axis=1)
    wb = _pad_cols(w_in[:, RWKV_IN:RWKV_IN + MLA_IN], MLA_IN_PAD)
    wc = w_in[:, RWKV_IN + MLA_IN:RWKV_IN + MLA_IN + CONV_IN]
    wd = w_in[:, RWKV_IN + MLA_IN + CONV_IN:]
    return wa.astype(BF16), wb.astype(BF16), wc.astype(BF16), wd.astype(BF16)


def _layout_rwkv_shift(shift):
    o = 3 * GROUP_W
    return jnp.concatenate([shift[:, :o],
                            _pad_cols(shift[:, o:o + RWKV_DECAY_LORA], RWKV_LORA_PAD),
                            _pad_cols(shift[:, o + RWKV_DECAY_LORA:o + RWKV_DECAY_LORA + RWKV_ICLR_LORA], RWKV_LORA_PAD),
                            shift[:, o + RWKV_DECAY_LORA + RWKV_ICLR_LORA:]], axis=1)


def _pad_lora_rows(w):
    return jnp.pad(w, ((0, 0), (0, RWKV_LORA_PAD - w.shape[1]), (0, 0))).astype(BF16)


def _layout_mla(w_uq, w_ukv):
    hp = MLA_HEAD_PAD
    dqk = MLA_NOPE + MLA_ROPE
    wq = jnp.concatenate([_pad_cols(w_uq[:, h * dqk:(h + 1) * dqk], hp) for h in range(MLA_HEADS)], axis=1)
    dkv = MLA_NOPE + MLA_V
    wk = jnp.concatenate([_pad_cols(w_ukv[:, h * dkv:h * dkv + MLA_NOPE], hp) for h in range(MLA_HEADS)], axis=1)
    wv = jnp.concatenate([w_ukv[:, h * dkv + MLA_NOPE:(h + 1) * dkv] for h in range(MLA_HEADS)], axis=1)
    return wq.astype(BF16), wk.astype(BF16), wv.astype(BF16)


def _rope_tables(seq_len, ctx_len, rot_dim, lead, width):
    rows = seq_len // GRID_W
    half = rot_dim // 2
    quarter = half // 2
    inv = ROPE_BASE ** (-jnp.arange(0, half, 2, dtype=F32) / half)
    row = jnp.repeat(jnp.arange(rows, dtype=F32), GRID_W)
    col = jnp.tile(jnp.arange(GRID_W, dtype=F32), rows)
    zeros = jnp.zeros((seq_len, quarter), F32)
    cos_parts, sa_parts, sb_parts = [], [], []
    for pos in (row, col):
        ang = pos[:, None] * inv[None, :]
        c, s = jnp.cos(ang), jnp.sin(ang)
        cos_parts += [c, c]
        sa_parts += [-s, zeros]
        sb_parts += [zeros, s]

    def table(parts, fill):
        body = jnp.concatenate(parts, axis=1)
        body = jnp.concatenate([jnp.full((seq_len, lead), fill, F32), body,
                                jnp.full((seq_len, width - lead - rot_dim), fill, F32)], axis=1)
        return jnp.concatenate([jnp.full((ctx_len, width), fill, F32), body], axis=0)

    return table(cos_parts, 1.0), table(sa_parts, 0.0), table(sb_parts, 0.0)


def _head_sum_matrix(width, head):
    i = jnp.arange(width) // head
    return (i[:, None] == i[None, :]).astype(BF16)


def _chunk_tri(tm, chunk):
    i = jnp.arange(tm)
    same = (i[:, None] // chunk) == (i[None, :] // chunk)
    lower = jnp.logical_and(same, i[None, :] <= i[:, None])
    upper = jnp.logical_and(same, i[None, :] >= i[:, None])
    return jnp.stack([lower, upper]).astype(BF16)


def _token_mixer(h, lw, tables, *, nb, tb, ctx_len):
    kw = dict(nb=nb, tb=tb)
    wa, wb, wc, wd = lw["w_in"]
    pa = _matmul(h, wa, F32, tn=512, **kw)
    pb = _matmul(h, wb, BF16, tn=MLA_IN_PAD, **kw)
    pc = _matmul(h, wc, BF16, tn=512, **kw)
    pd = _matmul(h, wd, BF16, tn=512, **kw)

    at, rt, bt, kt, wcum, v, gate, bonus = _rwkv_features(
        pa, lw["rwkv_shift"], lw["rwkv_w0"], lw["rwkv_w_up"], lw["rwkv_a0"], lw["rwkv_a_up"], lw["rwkv_g_up"],
        lw["rwkv_vecs"], tables["ones_bd"], tables["tri"], ctx_len=ctx_len, **kw)
    yf, yb_ = _rwkv_scan(at, rt, bt, kt, wcum, v, tables["scan_masks"], tables["same_head"],
                         ctx_len=ctx_len, feat_tm=256, **kw)
    ya = _rwkv_finish(yf, yb_, gate, bonus, lw["rwkv_vecs"], tables["ones_bd"], **kw)

    wq, wk, wv = lw["mla_w"]
    q, k, vv = _mla_project(pb, lw["mla_q_norm"], lw["mla_kv_norm"], wq, wk, wv, *tables["mla_rope"], **kw)
    yb = _mla_attention(q, k, vv, ctx_len=ctx_len, **kw)

    yc = _conv_mixer(pc, lw["conv_w"], ctx_len=ctx_len, **kw)
    yd = _retention(pd, lw["ret_decay"], lw["ret_gn_g"], *tables["ret_rope"], ctx_len=ctx_len, **kw)
    return ya, yb, yc, yd


def kernel(x, c, ctx, c_ctx, mod_w, mod_b, norm_g, w_in, rwkv_shift, rwkv_w0, rwkv_w_up, rwkv_a0, rwkv_a_up,
           rwkv_g_up, rwkv_vecs, mla_q_norm, mla_kv_norm, mla_w_uq, mla_w_ukv, conv_w, ret_decay, ret_gn_g, w_out,
           mlp_w_up, mlp_conv, mlp_w_down):
    nb, seq_len, d = x.shape
    ctx_len = ctx.shape[1]
    depth = mod_w.shape[0]
    tb = ctx_len + seq_len
    assert nb + 1 <= 8 and d == D_MODEL
    kw = dict(nb=nb, tb=tb, ctx_len=ctx_len)

    tables = {
        "ones_bd": _head_sum_matrix(GROUP_W, RWKV_HEAD),
        "tri": _chunk_tri(256, RWKV_CHUNK),
        "scan_masks": _rwkv_scan_masks(),
        "same_head": _head_sum_matrix(RWKV_QUAD, RWKV_HEAD),
        "mla_rope": _rope_tables(seq_len, ctx_len, MLA_ROPE, MLA_NOPE, MLA_HEAD_PAD),
        "ret_rope": _rope_tables(seq_len, ctx_len, RET_HEAD, 0, RET_HEAD),
    }

    c_pad = jnp.concatenate([c, c_ctx[None, :], jnp.zeros((8 - nb - 1, d), F32)], axis=0)
    mods = _modulation(c_pad, mod_w, mod_b).reshape(depth, 8, N_MOD, d)
    m_lat = mods[:, :nb]
    m_ctx = mods[:, nb]

    xs = jnp.concatenate([ctx, x], axis=1).reshape(nb * tb, d)
    h = _prologue(xs, m_ctx[0], m_lat[0], norm_g[0, 0:1], **kw)

    for l in range(depth):
        lw = {
            "w_in": _layout_w_in(w_in[l]),
            "rwkv_shift": _layout_rwkv_shift(rwkv_shift[l]),
            "rwkv_w0": rwkv_w0[l], "rwkv_w_up": _pad_lora_rows(rwkv_w_up[l]),
            "rwkv_a0": rwkv_a0[l], "rwkv_a_up": _pad_lora_rows(rwkv_a_up[l]),
            "rwkv_g_up": rwkv_g_up[l].astype(BF16), "rwkv_vecs": rwkv_vecs[l],
            "mla_q_norm": mla_q_norm[l][None, :], "mla_kv_norm": mla_kv_norm[l][None, :],
            "mla_w": _layout_mla(mla_w_uq[l], mla_w_ukv[l]),
            "conv_w": conv_w[l],
            "ret_decay": jnp.broadcast_to(
                jnp.pad(ret_decay[l].T, ((0, 0), (0, 6)))[:, :, None], (RET_HEADS, 8, 128)),
            "ret_gn_g": ret_gn_g[l][None, :],
        }
        ya, yb, yc, yd = _token_mixer(h, lw, tables, **kw)

        xs, h = _out_proj(ya, yb, yc, yd, w_out[l].astype(BF16), xs,
                          m_ctx[l][jnp.array([2, 3, 4])], m_lat[l][:, jnp.array([2, 3, 4])],
                          norm_g[l, 1:3], **kw)
        act = _ffn_up(h, mlp_w_up[l], mlp_conv[l], **kw)
        nl = min(l + 1, depth - 1)
        mc3 = jnp.stack([m_ctx[l][5], m_ctx[nl][0], m_ctx[nl][1]])
        mb3 = jnp.stack([m_lat[l][:, 5], m_lat[nl][:, 0], m_lat[nl][:, 1]], axis=1)
        g2 = jnp.stack([norm_g[l, 3], norm_g[nl, 0]])
        xs, h = _ffn_down(act, mlp_w_down[l].astype(BF16), xs, mc3, mb3, g2, **kw)

    return xs.reshape(nb, tb, d)[:, ctx_len:]
```

```python
import functools
import math

import jax
import jax.numpy as jnp
from jax import lax
from jax.experimental import pallas as pl
from jax.experimental.pallas import tpu as pltpu

F32 = jnp.float32
BF16 = jnp.bfloat16

D_MODEL = 2048
GRID_W = 64
GROUP_W = 512
N_MOD = 6
NORM_EPS = 1e-6
ROPE_BASE = 10000.0

RWKV_HEAD = 64
RWKV_HEADS = GROUP_W // RWKV_HEAD
RWKV_DECAY_LORA = 96
RWKV_ICLR_LORA = 96
RWKV_GATE_LORA = 256
RWKV_IN = 3 * GROUP_W + RWKV_DECAY_LORA + RWKV_ICLR_LORA + RWKV_GATE_LORA
RWKV_LORA_PAD = 128
RWKV_IN_PAD = 3 * GROUP_W + 2 * RWKV_LORA_PAD + RWKV_GATE_LORA
RWKV_DECAY_SCALE = math.exp(-0.5)
RWKV_GN_EPS = 64e-5
RWKV_CHUNK = 64

MLA_HEADS = 4
MLA_NOPE = 128
MLA_ROPE = 64
MLA_V = 128
MLA_Q_RANK = 384
MLA_KV_RANK = 256
MLA_IN = MLA_Q_RANK + MLA_KV_RANK + MLA_ROPE
MLA_IN_PAD = 768
MLA_HEAD_PAD = 256
MLA_SCALE = (MLA_NOPE + MLA_ROPE) ** -0.5

CONV_IN = 3 * GROUP_W

RET_HEADS = 4
RET_HEAD = 128
RET_IN = 4 * GROUP_W
GN_EPS = 1e-5

D_FF = 5632
VMEM_LIMIT = 56 * 1024 * 1024


def _cparams(*sem):
    return pltpu.CompilerParams(dimension_semantics=sem, vmem_limit_bytes=VMEM_LIMIT)


def _pick_tile(n, target, mult=16):
    best = None
    for t in range(mult, min(n, target) + 1, mult):
        if n % t == 0:
            best = t
    assert best is not None, (n, target)
    return best


def _dot(a, b):
    return jnp.dot(a.astype(BF16), b.astype(BF16), preferred_element_type=F32)


def _dot_nt(a, b):
    return lax.dot_general(a.astype(BF16), b.astype(BF16), (((1,), (1,)), ((), ())),
                           preferred_element_type=F32)


def _dot_tn(a, b):
    return lax.dot_general(a.astype(BF16), b.astype(BF16), (((0,), (0,)), ((), ())),
                           preferred_element_type=F32)


def _split3(x):
    hi = x.astype(BF16)
    r1 = x - hi.astype(F32)
    mid = r1.astype(BF16)
    lo = (r1 - mid.astype(F32)).astype(BF16)
    return hi, mid, lo


def _dot_wide_rhs(m, x):
    hi, mid, lo = _split3(x)
    return (jnp.dot(m, hi, preferred_element_type=F32) + jnp.dot(m, mid, preferred_element_type=F32)
            + jnp.dot(m, lo, preferred_element_type=F32))


def _dot_wide_lhs(x, m):
    hi, mid, lo = _split3(x)
    return (jnp.dot(hi, m, preferred_element_type=F32) + jnp.dot(mid, m, preferred_element_type=F32)
            + jnp.dot(lo, m, preferred_element_type=F32))


def _sigmoid(x):
    return 1.0 / (1.0 + jnp.exp(-x))


def _silu(x):
    return x * _sigmoid(x)


def _rms(x, g):
    ms = jnp.mean(x * x, axis=-1, keepdims=True)
    return x * lax.rsqrt(ms + NORM_EPS) * g


def _ctx_rows(tile_idx, tm, ctx_len):
    rows = tile_idx * tm + lax.broadcasted_iota(jnp.int32, (tm, 1), 0)
    return rows < ctx_len


def _mod_row(is_ctx, mc_ref, mb_ref, k):
    return jnp.where(is_ctx, mc_ref[k:k + 1, :], mb_ref[0, k:k + 1, :])


def _shift_rows(u, first_row, last_row):
    n = u.shape[0]
    rows = lax.broadcasted_iota(jnp.int32, (n, 1), 0)
    prev = jnp.where(rows == 0, first_row, pltpu.roll(u, 1, 0))
    nxt = jnp.where(rows == n - 1, last_row, pltpu.roll(u, n - 1, 0))
    return prev, nxt


def _rope(x, cos, sa, sb, half):
    n = x.shape[-1]
    return x * cos + pltpu.roll(x, n - half, 1) * sa + pltpu.roll(x, half, 1) * sb


def _mod_kernel(c_ref, w_ref, b_ref, o_ref):
    o_ref[0] = _dot(_silu(c_ref[...]), w_ref[0]) + b_ref[0]


def _modulation(c_pad, mod_w, mod_b):
    depth, d, n = mod_w.shape
    tn = 1024
    return pl.pallas_call(
        _mod_kernel,
        grid=(depth, n // tn),
        in_specs=[pl.BlockSpec((8, d), lambda l, j: (0, 0)),
                  pl.BlockSpec((1, d, tn), lambda l, j: (l, 0, j)),
                  pl.BlockSpec((1, 1, tn), lambda l, j: (l, 0, j))],
        out_specs=pl.BlockSpec((1, 8, tn), lambda l, j: (l, 0, j)),
        out_shape=jax.ShapeDtypeStruct((depth, 8, n), F32),
        compiler_params=_cparams("arbitrary", "arbitrary"),
    )(c_pad, mod_w, mod_b.reshape(depth, 1, n))


def _prologue_kernel(ctx_ref, x_ref, mc_ref, mb_ref, g_ref, xs_ref, h_ref, *, tm, ctx_len):
    t = pl.program_id(1)
    is_ctx = _ctx_rows(t, tm, ctx_len)
    x = jnp.where(t * tm < ctx_len, ctx_ref[...], x_ref[...])
    xs_ref[...] = x
    h = _rms(x, g_ref[...]) * (1.0 + _mod_row(is_ctx, mc_ref, mb_ref, 1)) + _mod_row(is_ctx, mc_ref, mb_ref, 0)
    h_ref[...] = h.astype(BF16)


def _prologue(ctx, x, mc, mb, g, *, nb, tb, ctx_len):
    d = x.shape[2]
    tm = 256
    assert ctx_len % tm == 0 and tb % tm == 0
    nt = tb // tm
    ct = ctx_len // tm
    row = lambda b, t: (b * nt + t, 0)
    return pl.pallas_call(
        functools.partial(_prologue_kernel, tm=tm, ctx_len=ctx_len),
        grid=(nb, nt),
        in_specs=[pl.BlockSpec((None, tm, d), lambda b, t: (b, jnp.minimum(t, ct - 1), 0)),
                  pl.BlockSpec((None, tm, d), lambda b, t: (b, jnp.maximum(t - ct, 0), 0)),
                  pl.BlockSpec((N_MOD, d), lambda b, t: (0, 0)),
                  pl.BlockSpec((1, N_MOD, d), lambda b, t: (b, 0, 0)),
                  pl.BlockSpec((1, d), lambda b, t: (0, 0))],
        out_specs=[pl.BlockSpec((tm, d), row), pl.BlockSpec((tm, d), row)],
        out_shape=[jax.ShapeDtypeStruct((nb * tb, d), F32), jax.ShapeDtypeStruct((nb * tb, d), BF16)],
        compiler_params=_cparams("arbitrary", "arbitrary"),
    )(ctx, x, mc, mb, g)


def _matmul_kernel(a_ref, w_ref, o_ref):
    o_ref[...] = jnp.dot(a_ref[...], w_ref[...], preferred_element_type=F32).astype(o_ref.dtype)


def _matmul(a, w, out_dtype, *, nb, tb, tn):
    k, n = w.shape
    assert n % tn == 0
    return pl.pallas_call(
        _matmul_kernel,
        grid=(nb, n // tn),
        in_specs=[pl.BlockSpec((tb, k), lambda b, j: (b, 0)),
                  pl.BlockSpec((k, tn), lambda b, j: (0, j))],
        out_specs=pl.BlockSpec((tb, tn), lambda b, j: (b, j)),
        out_shape=jax.ShapeDtypeStruct((a.shape[0], n), out_dtype),
        compiler_params=_cparams("arbitrary", "arbitrary"),
    )(a, w)


def _rwkv_feat_kernel(p_ref, pprev_ref, pnext_ref, shift_ref, w0_ref, wup_ref, a0_ref, aup_ref, gup_ref,
                      vecs_ref, ones_ref, tri_ref,
                      at_ref, rt_ref, bt_ref, kt_ref, wc_ref, v_ref, gate_ref, bonus_ref,
                      *, tm, ctx_tiles, nt):
    t = pl.program_id(1)
    first = jnp.logical_or(t == 0, t == ctx_tiles)
    last = jnp.logical_or(t == ctx_tiles - 1, t == nt - 1)
    u = p_ref[...]
    halo_prev = jnp.where(first, 0.0, pprev_ref[7:8, :])
    halo_next = jnp.where(last, 0.0, pnext_ref[0:1, :])
    prev, nxt = _shift_rows(u, halo_prev, halo_next)
    p = u + shift_ref[0:1, :] * (prev - u) + shift_ref[1:2, :] * (nxt - u)

    gw = GROUP_W
    r = p[:, 0:gw]
    k = p[:, gw:2 * gw]
    v = p[:, 2 * gw:3 * gw]
    wd = p[:, 3 * gw:3 * gw + RWKV_LORA_PAD]
    ad = p[:, 3 * gw + RWKV_LORA_PAD:3 * gw + 2 * RWKV_LORA_PAD]
    gd = p[:, 3 * gw + 2 * RWKV_LORA_PAD:]
    k_k = vecs_ref[0:1, :]
    k_a = vecs_ref[1:2, :]
    r_k = vecs_ref[2:3, :]
    ones = ones_ref[...]

    kk = k * k_k
    kk = kk * lax.rsqrt(_dot_wide_lhs(kk * kk, ones) + 1e-12)
    v_ref[...] = v.astype(BF16)
    gate_ref[...] = _dot(_sigmoid(gd), gup_ref[...])
    bonus_ref[...] = _dot_wide_lhs(r * k * r_k, ones) * v

    tanh_wd = jnp.tanh(wd)
    nch = tm // RWKV_CHUNK
    for d in range(2):
        lw = -RWKV_DECAY_SCALE * _sigmoid(w0_ref[d:d + 1, :] + _dot(tanh_wd, wup_ref[d]))
        l_inc = _dot_wide_rhs(tri_ref[d], lw)
        l_exc = l_inc - lw
        asig = _sigmoid(a0_ref[d:d + 1, :] + _dot(ad, aup_ref[d]))
        k_d = k * (1.0 + (asig - 1.0) * k_a)
        e_inc = jnp.exp(l_inc)
        e_neg = jnp.exp(-l_inc)
        at_ref[d] = (-kk * jnp.exp(l_exc)).astype(BF16)
        rt_ref[d] = (r * e_inc).astype(BF16)
        bt_ref[d] = (kk * asig * e_neg).astype(BF16)
        kt_ref[d] = (k_d * e_neg).astype(BF16)
        end = RWKV_CHUNK - 1 if d == 0 else 0
        rows = [e_inc[c * RWKV_CHUNK + end:c * RWKV_CHUNK + end + 1, :] for c in range(nch)]
        rows.append(jnp.zeros((8 - nch, gw), F32))
        wc_ref[d] = jnp.concatenate(rows, axis=0)


def _rwkv_features(pa, shift, w0, wup, a0, aup, gup, vecs, ones_bd, tri, *, nb, tb, ctx_len):
    r_tot = pa.shape[0]
    tm = 256
    assert ctx_len % tm == 0 and tb % tm == 0
    nt = tb // tm
    hb = tm // 8
    nblk8 = r_tot // 8
    gw = GROUP_W
    row = lambda b, t: (b * nt + t, 0)
    drow = lambda b, t: (0, b * nt + t, 0)
    const2 = lambda b, t: (0, 0)
    const3 = lambda b, t: (0, 0, 0)
    feat = jax.ShapeDtypeStruct((2, r_tot, gw), BF16)
    return pl.pallas_call(
        functools.partial(_rwkv_feat_kernel, tm=tm, ctx_tiles=ctx_len // tm, nt=nt),
        grid=(nb, nt),
        in_specs=[pl.BlockSpec((tm, RWKV_IN_PAD), row),
                  pl.BlockSpec((8, RWKV_IN_PAD), lambda b, t: (jnp.maximum((b * nt + t) * hb - 1, 0), 0)),
                  pl.BlockSpec((8, RWKV_IN_PAD), lambda b, t: (jnp.minimum((b * nt + t + 1) * hb, nblk8 - 1), 0)),
                  pl.BlockSpec((2, RWKV_IN_PAD), const2),
                  pl.BlockSpec((2, gw), const2),
                  pl.BlockSpec((2, RWKV_LORA_PAD, gw), const3),
                  pl.BlockSpec((2, gw), const2),
                  pl.BlockSpec((2, RWKV_LORA_PAD, gw), const3),
                  pl.BlockSpec((RWKV_GATE_LORA, gw), const2),
                  pl.BlockSpec((5, gw), const2),
                  pl.BlockSpec((gw, gw), const2),
                  pl.BlockSpec((2, tm, tm), const3)],
        out_specs=[pl.BlockSpec((2, tm, gw), drow)] * 4
        + [pl.BlockSpec((2, 8, gw), drow),
           pl.BlockSpec((tm, gw), row), pl.BlockSpec((tm, gw), row), pl.BlockSpec((tm, gw), row)],
        out_shape=[feat, feat, feat, feat,
                   jax.ShapeDtypeStruct((2, r_tot // tm * 8, gw), F32),
                   jax.ShapeDtypeStruct((r_tot, gw), BF16),
                   jax.ShapeDtypeStruct((r_tot, gw), F32),
                   jax.ShapeDtypeStruct((r_tot, gw), F32)],
        compiler_params=_cparams("arbitrary", "arbitrary"),
    )(pa, pa, pa, shift, w0, wup, a0, aup, gup, vecs, ones_bd, tri)


RWKV_QUAD = 4 * RWKV_HEAD
RWKV_STEP_CHUNKS = 2
RWKV_INV_LEVELS = 6
M_STRICT, M_INCL, M_LEVEL0, M_EYE, M_SAME = 0, 1, 2, 2 + RWKV_INV_LEVELS, 3 + RWKV_INV_LEVELS
RWKV_N_MASKS = 4 + RWKV_INV_LEVELS


def _rwkv_scan_masks():
    n, c = RWKV_QUAD, RWKV_CHUNK
    r = jnp.arange(n)[:, None]
    col = jnp.arange(n)[None, :]
    same = (r // c) == (col // c)
    t, j = r % c, col % c
    out = []
    for d in range(2):
        before = (j < t) if d == 0 else (j > t)
        ms = [same & before, same & (before | (j == t))]
        for lvl in range(RWKV_INV_LEVELS):
            s = 2 ** lvl
            blk = (r // (2 * s)) == (col // (2 * s))
            late_r = (r % (2 * s)) >= s
            late_c = (col % (2 * s)) >= s
            ms.append(blk & ((late_r & ~late_c) if d == 0 else (~late_r & late_c)))
        ms += [r == col, same]
        out.append(jnp.stack(ms))
    return jnp.stack(out).astype(F32)


def _head_stack(x):
    hd = RWKV_HEAD
    return jnp.concatenate([x[:, h * hd:(h + 1) * hd] for h in range(4)], axis=0)


def _head_unstack(x):
    c = RWKV_CHUNK
    return jnp.concatenate([x[h * c:(h + 1) * c, :] for h in range(4)], axis=1)


def _rwkv_scan_kernel(atf_ref, rtf_ref, btf_ref, ktf_ref, vf_ref, wcf_ref,
                      atb_ref, rtb_ref, btb_ref, ktb_ref, vb_ref, wcb_ref, mask_ref, same_ref,
                      yf_ref, yb_ref, s_ref, *, n_groups, ctx_groups, per_tile):
    i = pl.program_id(1)

    @pl.when(i == 0)
    def _():
        s_ref[...] = jnp.zeros_like(s_ref)

    nsub = RWKV_STEP_CHUNKS
    c_len = RWKV_CHUNK
    g_bwd = jnp.where(i < ctx_groups, ctx_groups - 1 - i, n_groups - 1 + ctx_groups - i)
    first_chunk = (i * nsub, g_bwd * nsub)
    ins = ((atf_ref, rtf_ref, btf_ref, ktf_ref, vf_ref, wcf_ref, yf_ref),
           (atb_ref, rtb_ref, btb_ref, ktb_ref, vb_ref, wcb_ref, yb_ref))
    qw = RWKV_QUAD
    n4 = 4 * c_len
    chains = [(d, q) for d in range(2) for q in range(GROUP_W // qw)]
    order = (list(range(nsub)), list(range(nsub - 1, -1, -1)))
    work = [(d, q, sub) for sub in range(nsub) for (d, q) in chains]

    def masked_stack(x):
        return jnp.concatenate([x, x, x, x], axis=0) * same_ref[...]

    def block_diag(x_st, d, m):
        return jnp.concatenate([x_st, x_st, x_st, x_st], axis=1) * mask_ref[d, m]

    st = {}
    for w in work:
        d, q, sub = w
        a_ref, r_ref, b_ref, k_ref, v_ref, _, _ = ins[d]
        sl = slice(q * qw, (q + 1) * qw)
        rows = slice(sub * c_len, (sub + 1) * c_len)
        b_, k_ = b_ref[0, rows, sl], k_ref[0, rows, sl]
        st[w] = dict(ms_ar=jnp.concatenate([masked_stack(a_ref[0, rows, sl]), masked_stack(r_ref[0, rows, sl])], axis=0),
                     ms_b=masked_stack(b_), ms_k=masked_stack(k_), bk=jnp.concatenate([b_, k_], axis=0),
                     v_st=_head_stack(v_ref[rows, sl]))
    for w in work:
        e = st[w]
        e["aa"] = _dot_nt(e["ms_ar"], e["bk"])
    for w in work:
        d = w[0]
        e = st[w]
        e["a_ab"] = block_diag(e["aa"][:n4, :c_len], d, M_STRICT)
        e["a_ak"] = block_diag(e["aa"][:n4, c_len:], d, M_STRICT)
        e["a_rb"] = block_diag(e["aa"][n4:, :c_len], d, M_INCL)
        e["a_rk"] = block_diag(e["aa"][n4:, c_len:], d, M_INCL)
        e["t"] = mask_ref[d, M_EYE] + e["a_ab"] * mask_ref[d, M_LEVEL0]
    for lvl in range(1, RWKV_INV_LEVELS):
        for w in work:
            e = st[w]
            e["w"] = _dot(e["a_ab"] * mask_ref[w[0], M_LEVEL0 + lvl], e["t"])
        for w in work:
            e = st[w]
            e["t"] = e["t"] + _dot(e["t"], e["w"])
    for w in work:
        e = st[w]
        e["akv"] = _dot(e["a_ak"], e["v_st"])
        e["rkv"] = _dot(e["a_rk"], e["v_st"])
        e["vtk"] = _dot_tn(e["v_st"], e["ms_k"])
    state = {ch: s_ref[ch[0], :, ch[1] * qw:(ch[1] + 1) * qw] for ch in chains}
    for step in range(nsub):
        cur = [(d, q, order[d][step]) for (d, q) in chains]
        for w in cur:
            e = st[w]
            e["xs"] = _dot_nt(e["ms_ar"], state[w[:2]])
        for w in cur:
            e = st[w]
            e["u"] = _dot(e["t"], e["xs"][:n4] + e["akv"])
        for w in cur:
            d, q, sub = w
            e = st[w]
            sl = slice(q * qw, (q + 1) * qw)
            y_st = e["xs"][n4:] + _dot(e["a_rb"], e["u"]) + e["rkv"]
            ins[d][6][sub * c_len:(sub + 1) * c_len, sl] = _head_unstack(y_st)
            wc = ins[d][5][0, pl.ds((first_chunk[d] + sub) % per_tile, 1), sl]
            state[(d, q)] = (state[(d, q)] + _dot_tn(e["u"], e["ms_b"]) + e["vtk"]) * wc
    for (d, q) in chains:
        s_ref[d, :, q * qw:(q + 1) * qw] = state[(d, q)]


def _rwkv_scan(at, rt, bt, kt, wc, v, masks, same, *, nb, tb, ctx_len, feat_tm):
    r_tot = v.shape[0]
    gw = GROUP_W
    nsub = RWKV_STEP_CHUNKS
    c = nsub * RWKV_CHUNK
    per_tile = feat_tm // RWKV_CHUNK
    assert tb % c == 0 and ctx_len % c == 0 and per_tile % nsub == 0
    n_groups = tb // c
    ctx_groups = ctx_len // c

    def cf(b, i):
        return b * n_groups + i

    def cb(b, i):
        return b * n_groups + jnp.where(i < ctx_groups, ctx_groups - 1 - i, n_groups - 1 + ctx_groups - i)

    def specs(d, cidx):
        feat = pl.BlockSpec((1, c, gw), lambda b, i: (d, cidx(b, i), 0))
        return [feat, feat, feat, feat,
                pl.BlockSpec((c, gw), lambda b, i: (cidx(b, i), 0)),
                pl.BlockSpec((1, 8, gw), lambda b, i: (d, cidx(b, i) * nsub // per_tile, 0))]

    return pl.pallas_call(
        functools.partial(_rwkv_scan_kernel, n_groups=n_groups, ctx_groups=ctx_groups, per_tile=per_tile),
        grid=(nb, n_groups),
        in_specs=specs(0, cf) + specs(1, cb)
        + [pl.BlockSpec((2, RWKV_N_MASKS, RWKV_QUAD, RWKV_QUAD), lambda b, i: (0, 0, 0, 0)),
           pl.BlockSpec((RWKV_QUAD, RWKV_QUAD), lambda b, i: (0, 0))],
        out_specs=[pl.BlockSpec((c, gw), lambda b, i: (cf(b, i), 0)),
                   pl.BlockSpec((c, gw), lambda b, i: (cb(b, i), 0))],
        out_shape=[jax.ShapeDtypeStruct((r_tot, gw), F32), jax.ShapeDtypeStruct((r_tot, gw), F32)],
        scratch_shapes=[pltpu.VMEM((2, RWKV_HEAD, gw), F32)],
        compiler_params=_cparams("arbitrary", "arbitrary"),
    )(at, rt, bt, kt, v, wc, at, rt, bt, kt, v, wc, masks, same)


def _rwkv_finish_kernel(yf_ref, yb_ref, gate_ref, bonus_ref, vecs_ref, ones_ref, o_ref):
    ones = ones_ref[...]
    y = yf_ref[...] + yb_ref[...]
    inv = 1.0 / RWKV_HEAD
    mu = _dot_wide_lhs(y, ones) * inv
    yc = y - mu
    var = _dot_wide_lhs(yc * yc, ones) * inv
    out = yc * lax.rsqrt(var + RWKV_GN_EPS) * vecs_ref[3:4, :] + vecs_ref[4:5, :] + bonus_ref[...]
    o_ref[...] = (out * gate_ref[...]).astype(o_ref.dtype)


def _rwkv_finish(yf, yb, gate, bonus, vecs, ones_bd, *, nb, tb):
    r_tot = gate.shape[0]
    gw = GROUP_W
    tm = _pick_tile(tb, 768)
    nt = tb // tm
    row = lambda b, t: (b * nt + t, 0)
    return pl.pallas_call(
        _rwkv_finish_kernel,
        grid=(nb, nt),
        in_specs=[pl.BlockSpec((tm, gw), row), pl.BlockSpec((tm, gw), row),
                  pl.BlockSpec((tm, gw), row), pl.BlockSpec((tm, gw), row),
                  pl.BlockSpec((5, gw), lambda b, t: (0, 0)),
                  pl.BlockSpec((gw, gw), lambda b, t: (0, 0))],
        out_specs=pl.BlockSpec((tm, gw), row),
        out_shape=jax.ShapeDtypeStruct((r_tot, gw), BF16),
        compiler_params=_cparams("arbitrary", "arbitrary"),
    )(yf, yb, gate, bonus, vecs, ones_bd)


def _mla_proj_kernel(p_ref, qn_ref, kvn_ref, wq_ref, wk_ref, wv_ref, cos_ref, sa_ref, sb_ref,
                     q_ref, k_ref, v_ref):
    p = p_ref[...].astype(F32)
    c_q = p[:, :MLA_Q_RANK]
    c_kv = p[:, MLA_Q_RANK:MLA_Q_RANK + MLA_KV_RANK]
    k_r = p[:, MLA_Q_RANK + MLA_KV_RANK:]
    n_q = _rms(c_q, qn_ref[...])
    n_kv = _rms(c_kv, kvn_ref[...])
    q = _dot(n_q, wq_ref[...])
    kx = _dot(n_kv, wk_ref[...])
    v_ref[...] = _dot(n_kv, wv_ref[...]).astype(BF16)
    cos, sa, sb = cos_ref[...], sa_ref[...], sb_ref[...]
    kr_blk = jnp.concatenate([jnp.zeros_like(k_r), k_r], axis=1)
    hp = MLA_HEAD_PAD
    for h in range(MLA_HEADS):
        sl = slice(h * hp, (h + 1) * hp)
        q_ref[:, sl] = _rope(q[:, sl], cos, sa, sb, MLA_ROPE // 4).astype(BF16)
        k_ref[:, sl] = _rope(kx[:, sl] + kr_blk, cos, sa, sb, MLA_ROPE // 4).astype(BF16)


def _mla_project(pb, qn, kvn, wq, wk, wv, cos, sa, sb, *, nb, tb):
    r_tot = pb.shape[0]
    tm = _pick_tile(tb, 768)
    nt = tb // tm
    row = lambda b, t: (b * nt + t, 0)
    trow = lambda b, t: (t, 0)
    c2 = lambda b, t: (0, 0)
    hw = MLA_HEADS * MLA_HEAD_PAD
    return pl.pallas_call(
        _mla_proj_kernel,
        grid=(nb, nt),
        in_specs=[pl.BlockSpec((tm, MLA_IN_PAD), row),
                  pl.BlockSpec((1, MLA_Q_RANK), c2), pl.BlockSpec((1, MLA_KV_RANK), c2),
                  pl.BlockSpec((MLA_Q_RANK, hw), c2), pl.BlockSpec((MLA_KV_RANK, hw), c2),
                  pl.BlockSpec((MLA_KV_RANK, GROUP_W), c2),
                  pl.BlockSpec((tm, MLA_HEAD_PAD), trow), pl.BlockSpec((tm, MLA_HEAD_PAD), trow),
                  pl.BlockSpec((tm, MLA_HEAD_PAD), trow)],
        out_specs=[pl.BlockSpec((tm, hw), row), pl.BlockSpec((tm, hw), row), pl.BlockSpec((tm, GROUP_W), row)],
        out_shape=[jax.ShapeDtypeStruct((r_tot, hw), BF16), jax.ShapeDtypeStruct((r_tot, hw), BF16),
                   jax.ShapeDtypeStruct((r_tot, GROUP_W), BF16)],
        compiler_params=_cparams("arbitrary", "arbitrary"),
    )(pb, qn, kvn, wq, wk, wv, cos, sa, sb)


def _softmax_pv(s, v):
    s = s * MLA_SCALE
    m = jnp.max(s, axis=-1, keepdims=True)
    e = jnp.exp(s - m)
    l = jnp.sum(e, axis=-1, keepdims=True)
    return _dot(e, v) / l


def _mla_attn_kernel(q_ref, k_ref, v_ref, o_ref, *, ctx_len, tq, n_qt):
    k_all = k_ref[...]
    v_all = v_ref[...]
    o_ref[0:ctx_len, :] = _softmax_pv(_dot_nt(q_ref[0:ctx_len, :], k_all[0:ctx_len]), v_all[0:ctx_len]).astype(BF16)

    for i in range(n_qt):
        rows = slice(ctx_len + i * tq, ctx_len + (i + 1) * tq)
        o_ref[rows, :] = _softmax_pv(_dot_nt(q_ref[rows, :], k_all), v_all).astype(BF16)


def _mla_attention(q, k, v, *, nb, tb, ctx_len):
    r_tot = q.shape[0]
    tq = 256
    assert (tb - ctx_len) % tq == 0 and ctx_len % tq == 0
    hp = MLA_HEAD_PAD
    return pl.pallas_call(
        functools.partial(_mla_attn_kernel, ctx_len=ctx_len, tq=tq, n_qt=(tb - ctx_len) // tq),
        grid=(nb, MLA_HEADS),
        in_specs=[pl.BlockSpec((tb, hp), lambda b, h: (b, h)),
                  pl.BlockSpec((tb, hp), lambda b, h: (b, h)),
                  pl.BlockSpec((tb, MLA_V), lambda b, h: (b, h))],
        out_specs=pl.BlockSpec((tb, MLA_V), lambda b, h: (b, h)),
        out_shape=jax.ShapeDtypeStruct((r_tot, GROUP_W), BF16),
        compiler_params=_cparams("arbitrary", "arbitrary"),
    )(q, k, v)


def _seq_edges(n, ctx_len):
    rows = lax.broadcasted_iota(jnp.int32, (n, 1), 0)
    starts = jnp.logical_or(rows == 0, rows == ctx_len)
    ends = jnp.logical_or(rows == ctx_len - 1, rows == n - 1)
    return starts, ends


def _dwconv3_rows(u, w_ref, starts, ends):
    n = u.shape[0]
    prev = jnp.where(starts, 0.0, pltpu.roll(u, 1, 0))
    nxt = jnp.where(ends, 0.0, pltpu.roll(u, n - 1, 0))
    return prev * w_ref[0:1, :] + u * w_ref[1:2, :] + nxt * w_ref[2:3, :]


def _conv_mix_kernel(b_ref, c_ref, u_ref, w_ref, o_ref, *, ctx_len):
    starts, ends = _seq_edges(b_ref.shape[0], ctx_len)
    z = c_ref[...].astype(F32) * u_ref[...].astype(F32)
    o_ref[...] = (b_ref[...].astype(F32) * _dwconv3_rows(z, w_ref, starts, ends)).astype(o_ref.dtype)


def _conv_mixer(pc, conv_w, *, nb, tb, ctx_len):
    r_tot = pc.shape[0]
    tc = 256
    nj = GROUP_W // tc
    return pl.pallas_call(
        functools.partial(_conv_mix_kernel, ctx_len=ctx_len),
        grid=(nb, nj),
        in_specs=[pl.BlockSpec((tb, tc), lambda b, j: (b, j)),
                  pl.BlockSpec((tb, tc), lambda b, j: (b, nj + j)),
                  pl.BlockSpec((tb, tc), lambda b, j: (b, 2 * nj + j)),
                  pl.BlockSpec((3, tc), lambda b, j: (0, j))],
        out_specs=pl.BlockSpec((tb, tc), lambda b, j: (b, j)),
        out_shape=jax.ShapeDtypeStruct((r_tot, GROUP_W), BF16),
        compiler_params=_cparams("arbitrary", "arbitrary"),
    )(pc, pc, pc, conv_w)


def _ret_kernel(q_ref, k_ref, v_ref, g_ref, dec_ref, gn_ref, cos_ref, sa_ref, sb_ref, o_ref,
                qs_ref, ks_ref, vs_ref, dm_ref, *, ctx_len, seq_len):
    blk = ctx_len
    nlat = seq_len // blk
    tb = ctx_len + seq_len
    cos, sa, sb = cos_ref[...], sa_ref[...], sb_ref[...]
    qs_ref[...] = _rope(q_ref[...].astype(F32), cos, sa, sb, RET_HEAD // 4).astype(BF16)
    kr = _rope(k_ref[...].astype(F32), cos, sa, sb, RET_HEAD // 4).astype(BF16)
    ks_ref[0:tb, :] = kr
    ks_ref[tb:tb + blk, :] = kr[0:blk]
    vs_ref[0:tb, :] = v_ref[...]
    vs_ref[tb:tb + blk, :] = v_ref[0:blk, :]
    lgf = -jnp.exp(dec_ref[0, 0:1, 0:1])
    lgb = -jnp.exp(dec_ref[0, 1:2, 0:1])
    gn = gn_ref[...]
    scale = RET_HEAD ** -0.5
    rc = (lax.broadcasted_iota(jnp.int32, (blk, blk), 0) - lax.broadcasted_iota(jnp.int32, (blk, blk), 1)).astype(F32)
    dm_ref[0] = jnp.exp(jnp.where(rc >= 0, rc * lgf, -rc * lgb))
    for dist in range(1, nlat + 1):
        dm_ref[dist] = jnp.exp((dist * blk + rc) * lgf)
        dm_ref[nlat + dist] = jnp.exp((dist * blk - rc) * lgb)

    def finish(o, g):
        mu = jnp.mean(o, axis=-1, keepdims=True)
        oc = o - mu
        var = jnp.mean(oc * oc, axis=-1, keepdims=True)
        return _silu(g) * (oc * lax.rsqrt(var + GN_EPS) * gn)

    s_c = _dot_nt(qs_ref[0:blk, :], ks_ref[0:blk, :]) * dm_ref[0]
    o_c = _dot(s_c, vs_ref[0:blk, :]) * scale
    o_ref[0:blk, :] = finish(o_c, g_ref[0:blk, :].astype(F32)).astype(o_ref.dtype)

    for i in range(nlat):
        rows = slice(ctx_len + i * blk, ctx_len + (i + 1) * blk)
        s = _dot_nt(qs_ref[rows, :], ks_ref[...])
        parts = []
        for jp in range(nlat + 2):
            j = jp - 1
            m = dm_ref[i - j] if j < i else (dm_ref[0] if j == i else dm_ref[nlat + j - i])
            parts.append((s[:, jp * blk:(jp + 1) * blk] * m).astype(BF16))
        o = jnp.dot(jnp.concatenate(parts, axis=1), vs_ref[...], preferred_element_type=F32) * scale
        o_ref[rows, :] = finish(o, g_ref[rows, :].astype(F32)).astype(o_ref.dtype)


def _retention(pd, dec, gn, cos, sa, sb, *, nb, tb, ctx_len):
    r_tot = pd.shape[0]
    hd = RET_HEAD
    nh = RET_HEADS
    seq_len = tb - ctx_len
    assert seq_len % ctx_len == 0
    nlat = seq_len // ctx_len
    tbl = pl.BlockSpec((tb, hd), lambda b, h: (0, 0))
    return pl.pallas_call(
        functools.partial(_ret_kernel, ctx_len=ctx_len, seq_len=seq_len),
        grid=(nb, nh),
        in_specs=[pl.BlockSpec((tb, hd), lambda b, h: (b, h)),
                  pl.BlockSpec((tb, hd), lambda b, h: (b, nh + h)),
                  pl.BlockSpec((tb, hd), lambda b, h: (b, 2 * nh + h)),
                  pl.BlockSpec((tb, hd), lambda b, h: (b, 3 * nh + h)),
                  pl.BlockSpec((1, 8, 128), lambda b, h: (h, 0, 0)),
                  pl.BlockSpec((1, hd), lambda b, h: (0, h)),
                  tbl, tbl, tbl],
        out_specs=pl.BlockSpec((tb, hd), lambda b, h: (b, h)),
        out_shape=jax.ShapeDtypeStruct((r_tot, GROUP_W), BF16),
        scratch_shapes=[pltpu.VMEM((tb, hd), BF16), pltpu.VMEM((tb + ctx_len, hd), BF16),
                        pltpu.VMEM((tb + ctx_len, hd), BF16), pltpu.VMEM((2 * nlat + 1, ctx_len, ctx_len), F32)],
        compiler_params=_cparams("arbitrary", "arbitrary"),
    )(pd, pd, pd, pd, dec, gn, cos, sa, sb)


def _residual_epilogue(y, x_ref, mc_ref, mb_ref, g_ref, is_ctx, x_out_ref, h_out_ref):
    x_new = x_ref[...] + _mod_row(is_ctx, mc_ref, mb_ref, 0) * _rms(y, g_ref[0:1, :])
    x_out_ref[...] = x_new
    h = _rms(x_new, g_ref[1:2, :]) * (1.0 + _mod_row(is_ctx, mc_ref, mb_ref, 2)) + _mod_row(is_ctx, mc_ref, mb_ref, 1)
    h_out_ref[...] = h.astype(BF16)


def _out_proj_kernel(ya_ref, yb_ref, yc_ref, yd_ref, w_ref, x_ref, mc_ref, mb_ref, g_ref, x_out_ref, h_out_ref,
                     *, tm, ctx_len):
    gw = GROUP_W
    y = (jnp.dot(ya_ref[...], w_ref[0:gw, :], preferred_element_type=F32)
         + jnp.dot(yb_ref[...], w_ref[gw:2 * gw, :], preferred_element_type=F32)
         + jnp.dot(yc_ref[...], w_ref[2 * gw:3 * gw, :], preferred_element_type=F32)
         + jnp.dot(yd_ref[...], w_ref[3 * gw:, :], preferred_element_type=F32))
    is_ctx = _ctx_rows(pl.program_id(1), tm, ctx_len)
    _residual_epilogue(y, x_ref, mc_ref, mb_ref, g_ref, is_ctx, x_out_ref, h_out_ref)


def _out_proj(ya, yb, yc, yd, w, x, mc3, mb3, g2, *, nb, tb, ctx_len):
    d = x.shape[1]
    gw = GROUP_W
    tm = _pick_tile(tb, 768)
    nt = tb // tm
    row = lambda b, t: (b * nt + t, 0)
    c2 = lambda b, t: (0, 0)
    return pl.pallas_call(
        functools.partial(_out_proj_kernel, tm=tm, ctx_len=ctx_len),
        grid=(nb, nt),
        in_specs=[pl.BlockSpec((tm, gw), row)] * 4
        + [pl.BlockSpec((4 * gw, d), c2, pipeline_mode=pl.Buffered(1)), pl.BlockSpec((tm, d), row),
           pl.BlockSpec((3, d), c2), pl.BlockSpec((1, 3, d), lambda b, t: (b, 0, 0)), pl.BlockSpec((2, d), c2)],
        out_specs=[pl.BlockSpec((tm, d), row), pl.BlockSpec((tm, d), row)],
        out_shape=[jax.ShapeDtypeStruct(x.shape, F32), jax.ShapeDtypeStruct(x.shape, BF16)],
        compiler_params=_cparams("arbitrary", "arbitrary"),
    )(ya, yb, yc, yd, w, x, mc3, mb3, g2)


FFN_PAD = 8


FFN_ROW_CHUNK = 768


def _ffn_up_kernel(h_ref, wg_ref, wv_ref, cg_ref, cv_ref, o_ref, ug_ref, uv_ref, *, ctx_len):
    n, tn = o_ref.shape
    rc, pad = FFN_ROW_CHUNK, FFN_PAD
    nch = n // rc
    wg = wg_ref[...].astype(BF16)
    wv = wv_ref[...].astype(BF16)
    for ref in (ug_ref, uv_ref):
        ref[0:pad, :] = jnp.zeros((pad, tn), F32)
        ref[pad + n:2 * pad + n, :] = jnp.zeros((pad, tn), F32)
    local = lax.broadcasted_iota(jnp.int32, (rc, 1), 0)

    def matmuls(c):
        h = h_ref[c * rc:(c + 1) * rc, :]
        ug_ref[pad + c * rc:pad + (c + 1) * rc, :] = jnp.dot(h, wg, preferred_element_type=F32)
        uv_ref[pad + c * rc:pad + (c + 1) * rc, :] = jnp.dot(h, wv, preferred_element_type=F32)

    def conv(ref, w_ref, c):
        r0 = c * rc
        prev = ref[pad + r0 - 1:pad + r0 - 1 + rc, :]
        cur = ref[pad + r0:pad + r0 + rc, :]
        nxt = ref[pad + r0 + 1:pad + r0 + 1 + rc, :]
        if r0 <= ctx_len < r0 + rc:
            prev = jnp.where(local == ctx_len - r0, 0.0, prev)
        if r0 <= ctx_len - 1 < r0 + rc:
            nxt = jnp.where(local == ctx_len - 1 - r0, 0.0, nxt)
        return prev * w_ref[0:1, :] + cur * w_ref[1:2, :] + nxt * w_ref[2:3, :]

    def epilogue(c):
        gate = conv(ug_ref, cg_ref, c)
        val = conv(uv_ref, cv_ref, c)
        o_ref[c * rc:(c + 1) * rc, :] = (_silu(gate) * val).astype(o_ref.dtype)

    for c in range(nch + 2):
        if c < nch:
            matmuls(c)
        if c >= 2:
            epilogue(c - 2)


def _ffn_up(h, w_up, w_conv, layer, *, nb, tb, ctx_len):
    r_tot, d = h.shape
    dff = w_up.shape[2] // 2
    tn = 256
    nj = dff // tn
    assert tb % FFN_ROW_CHUNK == 0
    buf = pltpu.VMEM((tb + 2 * FFN_PAD, tn), F32)
    return pl.pallas_call(
        functools.partial(_ffn_up_kernel, ctx_len=ctx_len),
        grid=(nb, nj),
        in_specs=[pl.BlockSpec((tb, d), lambda b, j: (b, 0)),
                  pl.BlockSpec((None, d, tn), lambda b, j: (layer, 0, j)),
                  pl.BlockSpec((None, d, tn), lambda b, j: (layer, 0, nj + j)),
                  pl.BlockSpec((None, 3, tn), lambda b, j: (layer, 0, j)),
                  pl.BlockSpec((None, 3, tn), lambda b, j: (layer, 0, nj + j))],
        out_specs=pl.BlockSpec((tb, tn), lambda b, j: (b, j)),
        out_shape=jax.ShapeDtypeStruct((r_tot, dff), BF16),
        scratch_shapes=[buf, buf],
        compiler_params=_cparams("arbitrary", "arbitrary"),
    )(h, w_up, w_up, w_conv, w_conv)


def _ffn_down_kernel(a_ref, w_ref, x_ref, mc_ref, mb_ref, g_ref, x_out_ref, h_out_ref, *, tm, ctx_len):
    y = jnp.dot(a_ref[...], w_ref[...], preferred_element_type=F32)
    is_ctx = _ctx_rows(pl.program_id(1), tm, ctx_len)
    _residual_epilogue(y, x_ref, mc_ref, mb_ref, g_ref, is_ctx, x_out_ref, h_out_ref)


def _ffn_down(act, w, x, mc3, mb3, g2, *, nb, tb, ctx_len):
    d = x.shape[1]
    dff = act.shape[1]
    tm = _pick_tile(tb, 384)
    nt = tb // tm
    row = lambda b, t: (b * nt + t, 0)
    c2 = lambda b, t: (0, 0)
    return pl.pallas_call(
        functools.partial(_ffn_down_kernel, tm=tm, ctx_len=ctx_len),
        grid=(nb, nt),
        in_specs=[pl.BlockSpec((tm, dff), row),
                  pl.BlockSpec((dff, d), c2, pipeline_mode=pl.Buffered(1)),
                  pl.BlockSpec((tm, d), row),
                  pl.BlockSpec((3, d), c2), pl.BlockSpec((1, 3, d), lambda b, t: (b, 0, 0)), pl.BlockSpec((2, d), c2)],
        out_specs=[pl.BlockSpec((tm, d), row), pl.BlockSpec((tm, d), row)],
        out_shape=[jax.ShapeDtypeStruct(x.shape, F32), jax.ShapeDtypeStruct(x.shape, BF16)],
        compiler_params=_cparams("arbitrary", "arbitrary"),
    )(act, w, x, mc3, mb3, g2)


def _pad_cols(w, n):
    return jnp.pad(w, ((0, 0), (0, n - w.shape[1])))


def _layout_w_in(w_in):
    gw = GROUP_W
    a = w_in[:, :RWKV_IN]
    o = 3 * gw
    wa = jnp.concatenate([a[:, :o],
                          _pad_cols(a[:, o:o + RWKV_DECAY_LORA], RWKV_LORA_PAD),
                          _pad_cols(a[:, o + RWKV_DECAY_LORA:o + RWKV_DECAY_LORA + RWKV_ICLR_LORA], RWKV_LORA_PAD),
                          a[:, o + RWKV_DECAY_LORA + RWKV_ICLR_LORA:]], axis=1)
    wb = _pad_cols(w_in[:, RWKV_IN:RWKV_IN + MLA_IN], MLA_IN_PAD)
    wc = w_in[:, RWKV_IN + MLA_IN:RWKV_IN + MLA_IN + CONV_IN]
    wd = w_in[:, RWKV_IN + MLA_IN + CONV_IN:]
    return wa.astype(BF16), wb.astype(BF16), wc.astype(BF16), wd.astype(BF16)


def _layout_rwkv_shift(shift):
    o = 3 * GROUP_W
    return jnp.concatenate([shift[:, :o],
                            _pad_cols(shift[:, o:o + RWKV_DECAY_LORA], RWKV_LORA_PAD),
                            _pad_cols(shift[:, o + RWKV_DECAY_LORA:o + RWKV_DECAY_LORA + RWKV_ICLR_LORA], RWKV_LORA_PAD),
                            shift[:, o + RWKV_DECAY_LORA + RWKV_ICLR_LORA:]], axis=1)


def _pad_lora_rows(w):
    return jnp.pad(w, ((0, 0), (0, RWKV_LORA_PAD - w.shape[1]), (0, 0))).astype(BF16)


def _layout_mla(w_uq, w_ukv):
    hp = MLA_HEAD_PAD
    dqk = MLA_NOPE + MLA_ROPE
    wq = jnp.concatenate([_pad_cols(w_uq[:, h * dqk:(h + 1) * dqk], hp) for h in range(MLA_HEADS)], axis=1)
    dkv = MLA_NOPE + MLA_V
    wk = jnp.concatenate([_pad_cols(w_ukv[:, h * dkv:h * dkv + MLA_NOPE], hp) for h in range(MLA_HEADS)], axis=1)
    wv = jnp.concatenate([w_ukv[:, h * dkv + MLA_NOPE:(h + 1) * dkv] for h in range(MLA_HEADS)], axis=1)
    return wq.astype(BF16), wk.astype(BF16), wv.astype(BF16)


def _rope_tables(seq_len, ctx_len, rot_dim, lead, width):
    rows = seq_len // GRID_W
    half = rot_dim // 2
    quarter = half // 2
    inv = ROPE_BASE ** (-jnp.arange(0, half, 2, dtype=F32) / half)
    row = jnp.repeat(jnp.arange(rows, dtype=F32), GRID_W)
    col = jnp.tile(jnp.arange(GRID_W, dtype=F32), rows)
    zeros = jnp.zeros((seq_len, quarter), F32)
    cos_parts, sa_parts, sb_parts = [], [], []
    for pos in (row, col):
        ang = pos[:, None] * inv[None, :]
        c, s = jnp.cos(ang), jnp.sin(ang)
        cos_parts += [c, c]
        sa_parts += [-s, zeros]
        sb_parts += [zeros, s]

    def table(parts, fill):
        body = jnp.concatenate(parts, axis=1)
        body = jnp.concatenate([jnp.full((seq_len, lead), fill, F32), body,
                                jnp.full((seq_len, width - lead - rot_dim), fill, F32)], axis=1)
        return jnp.concatenate([jnp.full((ctx_len, width), fill, F32), body], axis=0)

    return table(cos_parts, 1.0), table(sa_parts, 0.0), table(sb_parts, 0.0)


def _head_sum_matrix(width, head):
    i = jnp.arange(width) // head
    return (i[:, None] == i[None, :]).astype(BF16)


def _chunk_tri(tm, chunk):
    i = jnp.arange(tm)
    same = (i[:, None] // chunk) == (i[None, :] // chunk)
    lower = jnp.logical_and(same, i[None, :] <= i[:, None])
    upper = jnp.logical_and(same, i[None, :] >= i[:, None])
    return jnp.stack([lower, upper]).astype(BF16)


def _token_mixer(h, lw, tables, *, nb, tb, ctx_len):
    kw = dict(nb=nb, tb=tb)
    wa, wb, wc, wd = lw["w_in"]
    pa = _matmul(h, wa, F32, tn=512, **kw)
    pb = _matmul(h, wb, BF16, tn=MLA_IN_PAD, **kw)
    pc = _matmul(h, wc, BF16, tn=512, **kw)
    pd = _matmul(h, wd, BF16, tn=512, **kw)

    at, rt, bt, kt, wcum, v, gate, bonus = _rwkv_features(
        pa, lw["rwkv_shift"], lw["rwkv_w0"], lw["rwkv_w_up"], lw["rwkv_a0"], lw["rwkv_a_up"], lw["rwkv_g_up"],
        lw["rwkv_vecs"], tables["ones_bd"], tables["tri"], ctx_len=ctx_len, **kw)
    yf, yb_ = _rwkv_scan(at, rt, bt, kt, wcum, v, tables["scan_masks"], tables["same_head"],
                         ctx_len=ctx_len, feat_tm=256, **kw)
    ya = _rwkv_finish(yf, yb_, gate, bonus, lw["rwkv_vecs"], tables["ones_bd"], **kw)

    wq, wk, wv = lw["mla_w"]
    q, k, vv = _mla_project(pb, lw["mla_q_norm"], lw["mla_kv_norm"], wq, wk, wv, *tables["mla_rope"], **kw)
    yb = _mla_attention(q, k, vv, ctx_len=ctx_len, **kw)

    yc = _conv_mixer(pc, lw["conv_w"], ctx_len=ctx_len, **kw)
    yd = _retention(pd, lw["ret_decay"], lw["ret_gn_g"], *tables["ret_rope"], ctx_len=ctx_len, **kw)
    return ya, yb, yc, yd


def kernel(x, c, ctx, c_ctx, mod_w, mod_b, norm_g, w_in, rwkv_shift, rwkv_w0, rwkv_w_up, rwkv_a0, rwkv_a_up,
           rwkv_g_up, rwkv_vecs, mla_q_norm, mla_kv_norm, mla_w_uq, mla_w_ukv, conv_w, ret_decay, ret_gn_g, w_out,
           mlp_w_up, mlp_conv, mlp_w_down):
    nb, seq_len, d = x.shape
    ctx_len = ctx.shape[1]
    depth = mod_w.shape[0]
    tb = ctx_len + seq_len
    assert nb + 1 <= 8 and d == D_MODEL
    kw = dict(nb=nb, tb=tb, ctx_len=ctx_len)

    tables = {
        "ones_bd": _head_sum_matrix(GROUP_W, RWKV_HEAD),
        "tri": _chunk_tri(256, RWKV_CHUNK),
        "scan_masks": _rwkv_scan_masks(),
        "same_head": _head_sum_matrix(RWKV_QUAD, RWKV_HEAD),
        "mla_rope": _rope_tables(seq_len, ctx_len, MLA_ROPE, MLA_NOPE, MLA_HEAD_PAD),
        "ret_rope": _rope_tables(seq_len, ctx_len, RET_HEAD, 0, RET_HEAD),
    }

    c_pad = jnp.concatenate([c, c_ctx[None, :], jnp.zeros((8 - nb - 1, d), F32)], axis=0)
    mods = _modulation(c_pad, mod_w, mod_b).reshape(depth, 8, N_MOD, d)
    m_lat = mods[:, :nb]
    m_ctx = mods[:, nb]

    xs, h = _prologue(ctx, x, m_ctx[0], m_lat[0], norm_g[0, 0:1], **kw)

    for l in range(depth):
        lw = {
            "w_in": _layout_w_in(w_in[l]),
            "rwkv_shift": _layout_rwkv_shift(rwkv_shift[l]),
            "rwkv_w0": rwkv_w0[l], "rwkv_w_up": _pad_lora_rows(rwkv_w_up[l]),
            "rwkv_a0": rwkv_a0[l], "rwkv_a_up": _pad_lora_rows(rwkv_a_up[l]),
            "rwkv_g_up": rwkv_g_up[l].astype(BF16), "rwkv_vecs": rwkv_vecs[l],
            "mla_q_norm": mla_q_norm[l][None, :], "mla_kv_norm": mla_kv_norm[l][None, :],
            "mla_w": _layout_mla(mla_w_uq[l], mla_w_ukv[l]),
            "conv_w": conv_w[l],
            "ret_decay": jnp.broadcast_to(
                jnp.pad(ret_decay[l].T, ((0, 0), (0, 6)))[:, :, None], (RET_HEADS, 8, 128)),
            "ret_gn_g": ret_gn_g[l][None, :],
        }
        ya, yb, yc, yd = _token_mixer(h, lw, tables, **kw)

        xs, h = _out_proj(ya, yb, yc, yd, w_out[l].astype(BF16), xs,
                          m_ctx[l][jnp.array([2, 3, 4])], m_lat[l][:, jnp.array([2, 3, 4])],
                          norm_g[l, 1:3], **kw)
        act = _ffn_up(h, mlp_w_up, mlp_conv, l, **kw)
        nl = min(l + 1, depth - 1)
        mc3 = jnp.stack([m_ctx[l][5], m_ctx[nl][0], m_ctx[nl][1]])
        mb3 = jnp.stack([m_lat[l][:, 5], m_lat[nl][:, 0], m_lat[nl][:, 1]], axis=1)
        g2 = jnp.stack([norm_g[l, 3], norm_g[nl, 0]])
        xs, h = _ffn_down(act, mlp_w_down[l].astype(BF16), xs, mc3, mb3, g2, **kw)

    return xs.reshape(nb, tb, d)[:, ctx_len:]
```

```python
import functools
import math

import jax
import jax.numpy as jnp
from jax import lax
from jax.experimental import pallas as pl
from jax.experimental.pallas import tpu as pltpu

F32 = jnp.float32
BF16 = jnp.bfloat16

D_MODEL = 2048
GRID_W = 64
GROUP_W = 512
N_MOD = 6
NORM_EPS = 1e-6
ROPE_BASE = 10000.0

RWKV_HEAD = 64
RWKV_HEADS = GROUP_W // RWKV_HEAD
RWKV_DECAY_LORA = 96
RWKV_ICLR_LORA = 96
RWKV_GATE_LORA = 256
RWKV_IN = 3 * GROUP_W + RWKV_DECAY_LORA + RWKV_ICLR_LORA + RWKV_GATE_LORA
RWKV_LORA_PAD = 128
RWKV_IN_PAD = 3 * GROUP_W + 2 * RWKV_LORA_PAD + RWKV_GATE_LORA
RWKV_DECAY_SCALE = math.exp(-0.5)
RWKV_GN_EPS = 64e-5
RWKV_CHUNK = 64

MLA_HEADS = 4
MLA_NOPE = 128
MLA_ROPE = 64
MLA_V = 128
MLA_Q_RANK = 384
MLA_KV_RANK = 256
MLA_IN = MLA_Q_RANK + MLA_KV_RANK + MLA_ROPE
MLA_IN_PAD = 768
MLA_HEAD_PAD = 256
MLA_SCALE = (MLA_NOPE + MLA_ROPE) ** -0.5

CONV_IN = 3 * GROUP_W

P_RWKV = 0
P_RET = RWKV_IN_PAD
P_CONV_B = P_RET + 4 * GROUP_W
P_MLA = P_CONV_B + GROUP_W
P_CONV_C = P_MLA + MLA_IN_PAD
P_CONV_U = P_CONV_C + GROUP_W
P_WIDTH = P_CONV_U + GROUP_W

RET_HEADS = 4
RET_HEAD = 128
RET_IN = 4 * GROUP_W
GN_EPS = 1e-5

D_FF = 5632
VMEM_LIMIT = 56 * 1024 * 1024


def _cparams(*sem):
    return pltpu.CompilerParams(dimension_semantics=sem, vmem_limit_bytes=VMEM_LIMIT)


def _pick_tile(n, target, mult=16):
    best = None
    for t in range(mult, min(n, target) + 1, mult):
        if n % t == 0:
            best = t
    assert best is not None, (n, target)
    return best


def _dot(a, b):
    return jnp.dot(a.astype(BF16), b.astype(BF16), preferred_element_type=F32)


def _dot_nt(a, b):
    return lax.dot_general(a.astype(BF16), b.astype(BF16), (((1,), (1,)), ((), ())),
                           preferred_element_type=F32)


def _dot_tn(a, b):
    return lax.dot_general(a.astype(BF16), b.astype(BF16), (((0,), (0,)), ((), ())),
                           preferred_element_type=F32)


def _split3(x):
    hi = x.astype(BF16)
    r1 = x - hi.astype(F32)
    mid = r1.astype(BF16)
    lo = (r1 - mid.astype(F32)).astype(BF16)
    return hi, mid, lo


def _dot_wide_rhs(m, x):
    hi, mid, lo = _split3(x)
    return (jnp.dot(m, hi, preferred_element_type=F32) + jnp.dot(m, mid, preferred_element_type=F32)
            + jnp.dot(m, lo, preferred_element_type=F32))


def _dot_wide_lhs(x, m):
    hi, mid, lo = _split3(x)
    return (jnp.dot(hi, m, preferred_element_type=F32) + jnp.dot(mid, m, preferred_element_type=F32)
            + jnp.dot(lo, m, preferred_element_type=F32))


def _sigmoid(x):
    return 1.0 / (1.0 + jnp.exp(-x))


def _silu(x):
    return x * _sigmoid(x)


def _rms(x, g):
    ms = jnp.mean(x * x, axis=-1, keepdims=True)
    return x * lax.rsqrt(ms + NORM_EPS) * g


def _ctx_rows(tile_idx, tm, ctx_len):
    rows = tile_idx * tm + lax.broadcasted_iota(jnp.int32, (tm, 1), 0)
    return rows < ctx_len


def _mod_row(is_ctx, mc_ref, mb_ref, k):
    return jnp.where(is_ctx, mc_ref[k:k + 1, :], mb_ref[0, k:k + 1, :])


def _shift_rows(u, first_row, last_row):
    n = u.shape[0]
    rows = lax.broadcasted_iota(jnp.int32, (n, 1), 0)
    prev = jnp.where(rows == 0, first_row, pltpu.roll(u, 1, 0))
    nxt = jnp.where(rows == n - 1, last_row, pltpu.roll(u, n - 1, 0))
    return prev, nxt


def _rope(x, cos, sa, sb, half):
    n = x.shape[-1]
    return x * cos + pltpu.roll(x, n - half, 1) * sa + pltpu.roll(x, half, 1) * sb


def _mod_kernel(c_ref, w_ref, b_ref, o_ref):
    o_ref[0] = _dot(_silu(c_ref[...]), w_ref[0]) + b_ref[0]


def _modulation(c_pad, mod_w, mod_b):
    depth, d, n = mod_w.shape
    tn = 1024
    return pl.pallas_call(
        _mod_kernel,
        grid=(depth, n // tn),
        in_specs=[pl.BlockSpec((8, d), lambda l, j: (0, 0)),
                  pl.BlockSpec((1, d, tn), lambda l, j: (l, 0, j)),
                  pl.BlockSpec((1, 1, tn), lambda l, j: (l, 0, j))],
        out_specs=pl.BlockSpec((1, 8, tn), lambda l, j: (l, 0, j)),
        out_shape=jax.ShapeDtypeStruct((depth, 8, n), F32),
        compiler_params=_cparams("arbitrary", "arbitrary"),
    )(c_pad, mod_w, mod_b.reshape(depth, 1, n))


def _prologue_kernel(ctx_ref, x_ref, mc_ref, mb_ref, g_ref, xs_ref, h_ref, *, tm, ctx_len):
    t = pl.program_id(1)
    is_ctx = _ctx_rows(t, tm, ctx_len)
    x = jnp.where(t * tm < ctx_len, ctx_ref[...], x_ref[...])
    xs_ref[...] = x
    h = _rms(x, g_ref[...]) * (1.0 + _mod_row(is_ctx, mc_ref, mb_ref, 1)) + _mod_row(is_ctx, mc_ref, mb_ref, 0)
    h_ref[...] = h.astype(BF16)


def _prologue(ctx, x, mc, mb, g, *, nb, tb, ctx_len):
    d = x.shape[2]
    tm = 256
    assert ctx_len % tm == 0 and tb % tm == 0
    nt = tb // tm
    ct = ctx_len // tm
    row = lambda b, t: (b * nt + t, 0)
    return pl.pallas_call(
        functools.partial(_prologue_kernel, tm=tm, ctx_len=ctx_len),
        grid=(nb, nt),
        in_specs=[pl.BlockSpec((None, tm, d), lambda b, t: (b, jnp.minimum(t, ct - 1), 0)),
                  pl.BlockSpec((None, tm, d), lambda b, t: (b, jnp.maximum(t - ct, 0), 0)),
                  pl.BlockSpec((N_MOD, d), lambda b, t: (0, 0)),
                  pl.BlockSpec((1, N_MOD, d), lambda b, t: (b, 0, 0)),
                  pl.BlockSpec((1, d), lambda b, t: (0, 0))],
        out_specs=[pl.BlockSpec((tm, d), row), pl.BlockSpec((tm, d), row)],
        out_shape=[jax.ShapeDtypeStruct((nb * tb, d), F32), jax.ShapeDtypeStruct((nb * tb, d), BF16)],
        compiler_params=_cparams("arbitrary", "arbitrary"),
    )(ctx, x, mc, mb, g)


def _matmul_kernel(a_ref, w_ref, o_ref):
    o_ref[...] = jnp.dot(a_ref[...], w_ref[...], preferred_element_type=F32).astype(o_ref.dtype)


def _matmul(a, w, out_dtype, *, nb, tb, tn):
    k, n = w.shape
    assert n % tn == 0
    return pl.pallas_call(
        _matmul_kernel,
        grid=(nb, n // tn),
        in_specs=[pl.BlockSpec((tb, k), lambda b, j: (b, 0)),
                  pl.BlockSpec((k, tn), lambda b, j: (0, j))],
        out_specs=pl.BlockSpec((tb, tn), lambda b, j: (b, j)),
        out_shape=jax.ShapeDtypeStruct((a.shape[0], n), out_dtype),
        compiler_params=_cparams("arbitrary", "arbitrary"),
    )(a, w)


HALO_ROWS = 16


def _rwkv_feat_kernel(p_ref, pprev_ref, pnext_ref, shift_ref, w0_ref, wup_ref, a0_ref, aup_ref, gup_ref,
                      vecs_ref, ones_ref, tri_ref,
                      at_ref, rt_ref, bt_ref, kt_ref, wc_ref, v_ref, gate_ref, bonus_ref,
                      *, tm, ctx_tiles, nt):
    t = pl.program_id(1)
    first = jnp.logical_or(t == 0, t == ctx_tiles)
    last = jnp.logical_or(t == ctx_tiles - 1, t == nt - 1)
    u = p_ref[...].astype(F32)
    halo_prev = jnp.where(first, 0.0, pprev_ref[HALO_ROWS - 1:HALO_ROWS, :].astype(F32))
    halo_next = jnp.where(last, 0.0, pnext_ref[0:1, :].astype(F32))
    prev, nxt = _shift_rows(u, halo_prev, halo_next)
    p = u + shift_ref[0:1, :] * (prev - u) + shift_ref[1:2, :] * (nxt - u)

    gw = GROUP_W
    r = p[:, 0:gw]
    k = p[:, gw:2 * gw]
    v = p[:, 2 * gw:3 * gw]
    wd = p[:, 3 * gw:3 * gw + RWKV_LORA_PAD]
    ad = p[:, 3 * gw + RWKV_LORA_PAD:3 * gw + 2 * RWKV_LORA_PAD]
    gd = p[:, 3 * gw + 2 * RWKV_LORA_PAD:]
    k_k = vecs_ref[0:1, :]
    k_a = vecs_ref[1:2, :]
    r_k = vecs_ref[2:3, :]
    ones = ones_ref[...]

    kk = k * k_k
    kk = kk * lax.rsqrt(_dot_wide_lhs(kk * kk, ones) + 1e-12)
    v_ref[...] = v.astype(BF16)
    gate_ref[...] = _dot(_sigmoid(gd), gup_ref[...])
    bonus_ref[...] = _dot_wide_lhs(r * k * r_k, ones) * v

    tanh_wd = jnp.tanh(wd)
    nch = tm // RWKV_CHUNK
    for d in range(2):
        lw = -RWKV_DECAY_SCALE * _sigmoid(w0_ref[d:d + 1, :] + _dot(tanh_wd, wup_ref[d]))
        l_inc = _dot_wide_rhs(tri_ref[d], lw)
        l_exc = l_inc - lw
        asig = _sigmoid(a0_ref[d:d + 1, :] + _dot(ad, aup_ref[d]))
        k_d = k * (1.0 + (asig - 1.0) * k_a)
        e_inc = jnp.exp(l_inc)
        e_neg = jnp.exp(-l_inc)
        at_ref[d] = (-kk * jnp.exp(l_exc)).astype(BF16)
        rt_ref[d] = (r * e_inc).astype(BF16)
        bt_ref[d] = (kk * asig * e_neg).astype(BF16)
        kt_ref[d] = (k_d * e_neg).astype(BF16)
        end = RWKV_CHUNK - 1 if d == 0 else 0
        rows = [e_inc[c * RWKV_CHUNK + end:c * RWKV_CHUNK + end + 1, :] for c in range(nch)]
        rows.append(jnp.zeros((8 - nch, gw), F32))
        wc_ref[d] = jnp.concatenate(rows, axis=0)


def _rwkv_features(pa, shift, w0, wup, a0, aup, gup, vecs, ones_bd, tri, *, nb, tb, ctx_len):
    r_tot = pa.shape[0]
    tm = 256
    assert ctx_len % tm == 0 and tb % tm == 0
    nt = tb // tm
    hb = tm // HALO_ROWS
    nhalo = r_tot // HALO_ROWS
    gw = GROUP_W
    row = lambda b, t: (b * nt + t, 0)
    drow = lambda b, t: (0, b * nt + t, 0)
    const2 = lambda b, t: (0, 0)
    const3 = lambda b, t: (0, 0, 0)
    feat = jax.ShapeDtypeStruct((2, r_tot, gw), BF16)
    return pl.pallas_call(
        functools.partial(_rwkv_feat_kernel, tm=tm, ctx_tiles=ctx_len // tm, nt=nt),
        grid=(nb, nt),
        in_specs=[pl.BlockSpec((tm, RWKV_IN_PAD), row),
                  pl.BlockSpec((HALO_ROWS, RWKV_IN_PAD), lambda b, t: (jnp.maximum((b * nt + t) * hb - 1, 0), 0)),
                  pl.BlockSpec((HALO_ROWS, RWKV_IN_PAD), lambda b, t: (jnp.minimum((b * nt + t + 1) * hb, nhalo - 1), 0)),
                  pl.BlockSpec((2, RWKV_IN_PAD), const2),
                  pl.BlockSpec((2, gw), const2),
                  pl.BlockSpec((2, RWKV_LORA_PAD, gw), const3),
                  pl.BlockSpec((2, gw), const2),
                  pl.BlockSpec((2, RWKV_LORA_PAD, gw), const3),
                  pl.BlockSpec((RWKV_GATE_LORA, gw), const2),
                  pl.BlockSpec((5, gw), const2),
                  pl.BlockSpec((gw, gw), const2),
                  pl.BlockSpec((2, tm, tm), const3)],
        out_specs=[pl.BlockSpec((2, tm, gw), drow)] * 4
        + [pl.BlockSpec((2, 8, gw), drow),
           pl.BlockSpec((tm, gw), row), pl.BlockSpec((tm, gw), row), pl.BlockSpec((tm, gw), row)],
        out_shape=[feat, feat, feat, feat,
                   jax.ShapeDtypeStruct((2, r_tot // tm * 8, gw), F32),
                   jax.ShapeDtypeStruct((r_tot, gw), BF16),
                   jax.ShapeDtypeStruct((r_tot, gw), F32),
                   jax.ShapeDtypeStruct((r_tot, gw), F32)],
        compiler_params=_cparams("arbitrary", "arbitrary"),
    )(pa, pa, pa, shift, w0, wup, a0, aup, gup, vecs, ones_bd, tri)


RWKV_QUAD = 4 * RWKV_HEAD
RWKV_STEP_CHUNKS = 2
RWKV_INV_LEVELS = 6
M_STRICT, M_INCL, M_LEVEL0, M_EYE, M_SAME = 0, 1, 2, 2 + RWKV_INV_LEVELS, 3 + RWKV_INV_LEVELS
RWKV_N_MASKS = 4 + RWKV_INV_LEVELS


def _rwkv_scan_masks():
    n, c = RWKV_QUAD, RWKV_CHUNK
    r = jnp.arange(n)[:, None]
    col = jnp.arange(n)[None, :]
    same = (r // c) == (col // c)
    t, j = r % c, col % c
    out = []
    for d in range(2):
        before = (j < t) if d == 0 else (j > t)
        ms = [same & before, same & (before | (j == t))]
        for lvl in range(RWKV_INV_LEVELS):
            s = 2 ** lvl
            blk = (r // (2 * s)) == (col // (2 * s))
            late_r = (r % (2 * s)) >= s
            late_c = (col % (2 * s)) >= s
            ms.append(blk & ((late_r & ~late_c) if d == 0 else (~late_r & late_c)))
        ms += [r == col, same]
        out.append(jnp.stack(ms))
    return jnp.stack(out).astype(F32)


def _head_stack(x):
    hd = RWKV_HEAD
    return jnp.concatenate([x[:, h * hd:(h + 1) * hd] for h in range(4)], axis=0)


def _head_unstack(x):
    c = RWKV_CHUNK
    return jnp.concatenate([x[h * c:(h + 1) * c, :] for h in range(4)], axis=1)


def _rwkv_scan_kernel(atf_ref, rtf_ref, btf_ref, ktf_ref, vf_ref, wcf_ref,
                      atb_ref, rtb_ref, btb_ref, ktb_ref, vb_ref, wcb_ref, mask_ref, same_ref,
                      yf_ref, yb_ref, s_ref, *, n_groups, ctx_groups, per_tile):
    i = pl.program_id(1)

    @pl.when(i == 0)
    def _():
        s_ref[...] = jnp.zeros_like(s_ref)

    nsub = RWKV_STEP_CHUNKS
    c_len = RWKV_CHUNK
    g_bwd = jnp.where(i < ctx_groups, ctx_groups - 1 - i, n_groups - 1 + ctx_groups - i)
    first_chunk = (i * nsub, g_bwd * nsub)
    ins = ((atf_ref, rtf_ref, btf_ref, ktf_ref, vf_ref, wcf_ref, yf_ref),
           (atb_ref, rtb_ref, btb_ref, ktb_ref, vb_ref, wcb_ref, yb_ref))
    qw = RWKV_QUAD
    n4 = 4 * c_len
    chains = [(d, q) for d in range(2) for q in range(GROUP_W // qw)]
    order = (list(range(nsub)), list(range(nsub - 1, -1, -1)))
    work = [(d, q, sub) for sub in range(nsub) for (d, q) in chains]

    def masked_stack(x):
        return jnp.concatenate([x, x, x, x], axis=0) * same_ref[...]

    def block_diag(x_st, d, m):
        return jnp.concatenate([x_st, x_st, x_st, x_st], axis=1) * mask_ref[d, m]

    st = {}
    for w in work:
        d, q, sub = w
        a_ref, r_ref, b_ref, k_ref, v_ref, _, _ = ins[d]
        sl = slice(q * qw, (q + 1) * qw)
        rows = slice(sub * c_len, (sub + 1) * c_len)
        b_, k_ = b_ref[0, rows, sl], k_ref[0, rows, sl]
        st[w] = dict(ms_ar=jnp.concatenate([masked_stack(a_ref[0, rows, sl]), masked_stack(r_ref[0, rows, sl])], axis=0),
                     ms_b=masked_stack(b_), ms_k=masked_stack(k_), bk=jnp.concatenate([b_, k_], axis=0),
                     v_st=_head_stack(v_ref[rows, sl]))
    for w in work:
        e = st[w]
        e["aa"] = _dot_nt(e["ms_ar"], e["bk"])
    for w in work:
        d = w[0]
        e = st[w]
        e["a_ab"] = block_diag(e["aa"][:n4, :c_len], d, M_STRICT)
        e["a_ak"] = block_diag(e["aa"][:n4, c_len:], d, M_STRICT)
        e["a_rb"] = block_diag(e["aa"][n4:, :c_len], d, M_INCL)
        e["a_rk"] = block_diag(e["aa"][n4:, c_len:], d, M_INCL)
        e["t"] = mask_ref[d, M_EYE] + e["a_ab"] * mask_ref[d, M_LEVEL0]
    for lvl in range(1, RWKV_INV_LEVELS):
        for w in work:
            e = st[w]
            e["w"] = _dot(e["a_ab"] * mask_ref[w[0], M_LEVEL0 + lvl], e["t"])
        for w in work:
            e = st[w]
            e["t"] = e["t"] + _dot(e["t"], e["w"])
    for w in work:
        e = st[w]
        e["akv"] = _dot(e["a_ak"], e["v_st"])
        e["rkv"] = _dot(e["a_rk"], e["v_st"])
        e["vtk"] = _dot_tn(e["v_st"], e["ms_k"])
    state = {ch: s_ref[ch[0], :, ch[1] * qw:(ch[1] + 1) * qw] for ch in chains}
    for step in range(nsub):
        cur = [(d, q, order[d][step]) for (d, q) in chains]
        for w in cur:
            e = st[w]
            e["xs"] = _dot_nt(e["ms_ar"], state[w[:2]])
        for w in cur:
            e = st[w]
            e["u"] = _dot(e["t"], e["xs"][:n4] + e["akv"])
        for w in cur:
            d, q, sub = w
            e = st[w]
            sl = slice(q * qw, (q + 1) * qw)
            y_st = e["xs"][n4:] + _dot(e["a_rb"], e["u"]) + e["rkv"]
            ins[d][6][sub * c_len:(sub + 1) * c_len, sl] = _head_unstack(y_st)
            wc = ins[d][5][0, pl.ds((first_chunk[d] + sub) % per_tile, 1), sl]
            state[(d, q)] = (state[(d, q)] + _dot_tn(e["u"], e["ms_b"]) + e["vtk"]) * wc
    for (d, q) in chains:
        s_ref[d, :, q * qw:(q + 1) * qw] = state[(d, q)]


def _rwkv_scan(at, rt, bt, kt, wc, v, masks, same, *, nb, tb, ctx_len, feat_tm):
    r_tot = v.shape[0]
    gw = GROUP_W
    nsub = RWKV_STEP_CHUNKS
    c = nsub * RWKV_CHUNK
    per_tile = feat_tm // RWKV_CHUNK
    assert tb % c == 0 and ctx_len % c == 0 and per_tile % nsub == 0
    n_groups = tb // c
    ctx_groups = ctx_len // c

    def cf(b, i):
        return b * n_groups + i

    def cb(b, i):
        return b * n_groups + jnp.where(i < ctx_groups, ctx_groups - 1 - i, n_groups - 1 + ctx_groups - i)

    def specs(d, cidx):
        feat = pl.BlockSpec((1, c, gw), lambda b, i: (d, cidx(b, i), 0))
        return [feat, feat, feat, feat,
                pl.BlockSpec((c, gw), lambda b, i: (cidx(b, i), 0)),
                pl.BlockSpec((1, 8, gw), lambda b, i: (d, cidx(b, i) * nsub // per_tile, 0))]

    return pl.pallas_call(
        functools.partial(_rwkv_scan_kernel, n_groups=n_groups, ctx_groups=ctx_groups, per_tile=per_tile),
        grid=(nb, n_groups),
        in_specs=specs(0, cf) + specs(1, cb)
        + [pl.BlockSpec((2, RWKV_N_MASKS, RWKV_QUAD, RWKV_QUAD), lambda b, i: (0, 0, 0, 0)),
           pl.BlockSpec((RWKV_QUAD, RWKV_QUAD), lambda b, i: (0, 0))],
        out_specs=[pl.BlockSpec((c, gw), lambda b, i: (cf(b, i), 0)),
                   pl.BlockSpec((c, gw), lambda b, i: (cb(b, i), 0))],
        out_shape=[jax.ShapeDtypeStruct((r_tot, gw), F32), jax.ShapeDtypeStruct((r_tot, gw), F32)],
        scratch_shapes=[pltpu.VMEM((2, RWKV_HEAD, gw), F32)],
        compiler_params=_cparams("arbitrary", "arbitrary"),
    )(at, rt, bt, kt, v, wc, at, rt, bt, kt, v, wc, masks, same)


def _rwkv_finish_kernel(yf_ref, yb_ref, gate_ref, bonus_ref, vecs_ref, ones_ref, o_ref):
    ones = ones_ref[...]
    y = yf_ref[...] + yb_ref[...]
    inv = 1.0 / RWKV_HEAD
    mu = _dot_wide_lhs(y, ones) * inv
    yc = y - mu
    var = _dot_wide_lhs(yc * yc, ones) * inv
    out = yc * lax.rsqrt(var + RWKV_GN_EPS) * vecs_ref[3:4, :] + vecs_ref[4:5, :] + bonus_ref[...]
    o_ref[...] = (out * gate_ref[...]).astype(o_ref.dtype)


def _rwkv_finish(yf, yb, gate, bonus, vecs, ones_bd, *, nb, tb):
    r_tot = gate.shape[0]
    gw = GROUP_W
    tm = _pick_tile(tb, 768)
    nt = tb // tm
    row = lambda b, t: (b * nt + t, 0)
    return pl.pallas_call(
        _rwkv_finish_kernel,
        grid=(nb, nt),
        in_specs=[pl.BlockSpec((tm, gw), row), pl.BlockSpec((tm, gw), row),
                  pl.BlockSpec((tm, gw), row), pl.BlockSpec((tm, gw), row),
                  pl.BlockSpec((5, gw), lambda b, t: (0, 0)),
                  pl.BlockSpec((gw, gw), lambda b, t: (0, 0))],
        out_specs=pl.BlockSpec((tm, gw), row),
        out_shape=jax.ShapeDtypeStruct((r_tot, gw), BF16),
        compiler_params=_cparams("arbitrary", "arbitrary"),
    )(yf, yb, gate, bonus, vecs, ones_bd)


def _mla_proj_kernel(p_ref, qn_ref, kvn_ref, wq_ref, wk_ref, wv_ref, cos_ref, sa_ref, sb_ref,
                     q_ref, k_ref, v_ref):
    p = p_ref[...].astype(F32)
    c_q = p[:, :MLA_Q_RANK]
    c_kv = p[:, MLA_Q_RANK:MLA_Q_RANK + MLA_KV_RANK]
    k_r = p[:, MLA_Q_RANK + MLA_KV_RANK:]
    n_q = _rms(c_q, qn_ref[...])
    n_kv = _rms(c_kv, kvn_ref[...])
    q = _dot(n_q, wq_ref[...])
    kx = _dot(n_kv, wk_ref[...])
    v_ref[...] = _dot(n_kv, wv_ref[...]).astype(BF16)
    cos, sa, sb = cos_ref[...], sa_ref[...], sb_ref[...]
    kr_blk = jnp.concatenate([jnp.zeros_like(k_r), k_r], axis=1)
    hp = MLA_HEAD_PAD
    for h in range(MLA_HEADS):
        sl = slice(h * hp, (h + 1) * hp)
        q_ref[:, sl] = _rope(q[:, sl], cos, sa, sb, MLA_ROPE // 4).astype(BF16)
        k_ref[:, sl] = _rope(kx[:, sl] + kr_blk, cos, sa, sb, MLA_ROPE // 4).astype(BF16)


def _mla_project(pb, qn, kvn, wq, wk, wv, cos, sa, sb, *, nb, tb):
    r_tot = pb.shape[0]
    tm = _pick_tile(tb, 768)
    nt = tb // tm
    row = lambda b, t: (b * nt + t, 0)
    trow = lambda b, t: (t, 0)
    c2 = lambda b, t: (0, 0)
    hw = MLA_HEADS * MLA_HEAD_PAD
    return pl.pallas_call(
        _mla_proj_kernel,
        grid=(nb, nt),
        in_specs=[pl.BlockSpec((tm, MLA_IN_PAD), lambda b, t: (b * nt + t, P_MLA // MLA_IN_PAD)),
                  pl.BlockSpec((1, MLA_Q_RANK), c2), pl.BlockSpec((1, MLA_KV_RANK), c2),
                  pl.BlockSpec((MLA_Q_RANK, hw), c2), pl.BlockSpec((MLA_KV_RANK, hw), c2),
                  pl.BlockSpec((MLA_KV_RANK, GROUP_W), c2),
                  pl.BlockSpec((tm, MLA_HEAD_PAD), trow), pl.BlockSpec((tm, MLA_HEAD_PAD), trow),
                  pl.BlockSpec((tm, MLA_HEAD_PAD), trow)],
        out_specs=[pl.BlockSpec((tm, hw), row), pl.BlockSpec((tm, hw), row), pl.BlockSpec((tm, GROUP_W), row)],
        out_shape=[jax.ShapeDtypeStruct((r_tot, hw), BF16), jax.ShapeDtypeStruct((r_tot, hw), BF16),
                   jax.ShapeDtypeStruct((r_tot, GROUP_W), BF16)],
        compiler_params=_cparams("arbitrary", "arbitrary"),
    )(pb, qn, kvn, wq, wk, wv, cos, sa, sb)


def _softmax_pv(s, v):
    s = s * MLA_SCALE
    m = jnp.max(s, axis=-1, keepdims=True)
    e = jnp.exp(s - m)
    l = jnp.sum(e, axis=-1, keepdims=True)
    return _dot(e, v) / l


def _mla_attn_kernel(q_ref, k_ref, v_ref, o_ref, *, ctx_len, tq, n_qt):
    k_all = k_ref[...]
    v_all = v_ref[...]
    o_ref[0:ctx_len, :] = _softmax_pv(_dot_nt(q_ref[0:ctx_len, :], k_all[0:ctx_len]), v_all[0:ctx_len]).astype(BF16)

    for i in range(n_qt):
        rows = slice(ctx_len + i * tq, ctx_len + (i + 1) * tq)
        o_ref[rows, :] = _softmax_pv(_dot_nt(q_ref[rows, :], k_all), v_all).astype(BF16)


def _mla_attention(q, k, v, *, nb, tb, ctx_len):
    r_tot = q.shape[0]
    tq = 256
    assert (tb - ctx_len) % tq == 0 and ctx_len % tq == 0
    hp = MLA_HEAD_PAD
    return pl.pallas_call(
        functools.partial(_mla_attn_kernel, ctx_len=ctx_len, tq=tq, n_qt=(tb - ctx_len) // tq),
        grid=(nb, MLA_HEADS),
        in_specs=[pl.BlockSpec((tb, hp), lambda b, h: (b, h)),
                  pl.BlockSpec((tb, hp), lambda b, h: (b, h)),
                  pl.BlockSpec((tb, MLA_V), lambda b, h: (b, h))],
        out_specs=pl.BlockSpec((tb, MLA_V), lambda b, h: (b, h)),
        out_shape=jax.ShapeDtypeStruct((r_tot, GROUP_W), BF16),
        compiler_params=_cparams("arbitrary", "arbitrary"),
    )(q, k, v)


def _seq_edges(n, ctx_len):
    rows = lax.broadcasted_iota(jnp.int32, (n, 1), 0)
    starts = jnp.logical_or(rows == 0, rows == ctx_len)
    ends = jnp.logical_or(rows == ctx_len - 1, rows == n - 1)
    return starts, ends


def _dwconv3_rows(u, w_ref, starts, ends):
    n = u.shape[0]
    prev = jnp.where(starts, 0.0, pltpu.roll(u, 1, 0))
    nxt = jnp.where(ends, 0.0, pltpu.roll(u, n - 1, 0))
    return prev * w_ref[0:1, :] + u * w_ref[1:2, :] + nxt * w_ref[2:3, :]


def _conv_mix_kernel(b_ref, c_ref, u_ref, w_ref, o_ref, *, ctx_len):
    starts, ends = _seq_edges(b_ref.shape[0], ctx_len)
    z = c_ref[...].astype(F32) * u_ref[...].astype(F32)
    o_ref[...] = (b_ref[...].astype(F32) * _dwconv3_rows(z, w_ref, starts, ends)).astype(o_ref.dtype)


def _conv_mixer(pc, conv_w, *, nb, tb, ctx_len):
    r_tot = pc.shape[0]
    tc = 256
    nj = GROUP_W // tc
    return pl.pallas_call(
        functools.partial(_conv_mix_kernel, ctx_len=ctx_len),
        grid=(nb, nj),
        in_specs=[pl.BlockSpec((tb, tc), lambda b, j: (b, P_CONV_B // tc + j)),
                  pl.BlockSpec((tb, tc), lambda b, j: (b, P_CONV_C // tc + j)),
                  pl.BlockSpec((tb, tc), lambda b, j: (b, P_CONV_U // tc + j)),
                  pl.BlockSpec((3, tc), lambda b, j: (0, j))],
        out_specs=pl.BlockSpec((tb, tc), lambda b, j: (b, j)),
        out_shape=jax.ShapeDtypeStruct((r_tot, GROUP_W), BF16),
        compiler_params=_cparams("arbitrary", "arbitrary"),
    )(pc, pc, pc, conv_w)


def _ret_kernel(q_ref, k_ref, v_ref, g_ref, dec_ref, gn_ref, cos_ref, sa_ref, sb_ref, o_ref,
                qs_ref, ks_ref, vs_ref, dm_ref, *, ctx_len, seq_len):
    blk = ctx_len
    nlat = seq_len // blk
    tb = ctx_len + seq_len
    cos, sa, sb = cos_ref[...], sa_ref[...], sb_ref[...]
    qs_ref[...] = _rope(q_ref[...].astype(F32), cos, sa, sb, RET_HEAD // 4).astype(BF16)
    kr = _rope(k_ref[...].astype(F32), cos, sa, sb, RET_HEAD // 4).astype(BF16)
    ks_ref[0:tb, :] = kr
    ks_ref[tb:tb + blk, :] = kr[0:blk]
    vs_ref[0:tb, :] = v_ref[...]
    vs_ref[tb:tb + blk, :] = v_ref[0:blk, :]
    lgf = -jnp.exp(dec_ref[0, 0:1, 0:1])
    lgb = -jnp.exp(dec_ref[0, 1:2, 0:1])
    gn = gn_ref[...]
    scale = RET_HEAD ** -0.5
    rc = (lax.broadcasted_iota(jnp.int32, (blk, blk), 0) - lax.broadcasted_iota(jnp.int32, (blk, blk), 1)).astype(F32)
    dm_ref[0] = jnp.exp(jnp.where(rc >= 0, rc * lgf, -rc * lgb))
    for dist in range(1, nlat + 1):
        dm_ref[dist] = jnp.exp((dist * blk + rc) * lgf)
        dm_ref[nlat + dist] = jnp.exp((dist * blk - rc) * lgb)

    def finish(o, g):
        mu = jnp.mean(o, axis=-1, keepdims=True)
        oc = o - mu
        var = jnp.mean(oc * oc, axis=-1, keepdims=True)
        return _silu(g) * (oc * lax.rsqrt(var + GN_EPS) * gn)

    s_c = _dot_nt(qs_ref[0:blk, :], ks_ref[0:blk, :]) * dm_ref[0]
    o_c = _dot(s_c, vs_ref[0:blk, :]) * scale
    o_ref[0:blk, :] = finish(o_c, g_ref[0:blk, :].astype(F32)).astype(o_ref.dtype)

    for i in range(nlat):
        rows = slice(ctx_len + i * blk, ctx_len + (i + 1) * blk)
        s = _dot_nt(qs_ref[rows, :], ks_ref[...])
        parts = []
        for jp in range(nlat + 2):
            j = jp - 1
            m = dm_ref[i - j] if j < i else (dm_ref[0] if j == i else dm_ref[nlat + j - i])
            parts.append((s[:, jp * blk:(jp + 1) * blk] * m).astype(BF16))
        o = jnp.dot(jnp.concatenate(parts, axis=1), vs_ref[...], preferred_element_type=F32) * scale
        o_ref[rows, :] = finish(o, g_ref[rows, :].astype(F32)).astype(o_ref.dtype)


def _retention(pd, dec, gn, cos, sa, sb, *, nb, tb, ctx_len):
    r_tot = pd.shape[0]
    hd = RET_HEAD
    nh = RET_HEADS
    seq_len = tb - ctx_len
    assert seq_len % ctx_len == 0
    nlat = seq_len // ctx_len
    tbl = pl.BlockSpec((tb, hd), lambda b, h: (0, 0))
    return pl.pallas_call(
        functools.partial(_ret_kernel, ctx_len=ctx_len, seq_len=seq_len),
        grid=(nb, nh),
        in_specs=[pl.BlockSpec((tb, hd), lambda b, h: (b, P_RET // hd + h)),
                  pl.BlockSpec((tb, hd), lambda b, h: (b, P_RET // hd + nh + h)),
                  pl.BlockSpec((tb, hd), lambda b, h: (b, P_RET // hd + 2 * nh + h)),
                  pl.BlockSpec((tb, hd), lambda b, h: (b, P_RET // hd + 3 * nh + h)),
                  pl.BlockSpec((1, 8, 128), lambda b, h: (h, 0, 0)),
                  pl.BlockSpec((1, hd), lambda b, h: (0, h)),
                  tbl, tbl, tbl],
        out_specs=pl.BlockSpec((tb, hd), lambda b, h: (b, h)),
        out_shape=jax.ShapeDtypeStruct((r_tot, GROUP_W), BF16),
        scratch_shapes=[pltpu.VMEM((tb, hd), BF16), pltpu.VMEM((tb + ctx_len, hd), BF16),
                        pltpu.VMEM((tb + ctx_len, hd), BF16), pltpu.VMEM((2 * nlat + 1, ctx_len, ctx_len), F32)],
        compiler_params=_cparams("arbitrary", "arbitrary"),
    )(pd, pd, pd, pd, dec, gn, cos, sa, sb)


def _residual_epilogue(y, x_ref, mc_ref, mb_ref, g_ref, is_ctx, x_out_ref, h_out_ref):
    x_new = x_ref[...] + _mod_row(is_ctx, mc_ref, mb_ref, 0) * _rms(y, g_ref[0:1, :])
    x_out_ref[...] = x_new
    h = _rms(x_new, g_ref[1:2, :]) * (1.0 + _mod_row(is_ctx, mc_ref, mb_ref, 2)) + _mod_row(is_ctx, mc_ref, mb_ref, 1)
    h_out_ref[...] = h.astype(BF16)


def _out_proj_kernel(ya_ref, yb_ref, yc_ref, yd_ref, w_ref, x_ref, mc_ref, mb_ref, g_ref, x_out_ref, h_out_ref,
                     *, tm, ctx_len):
    gw = GROUP_W
    y = (jnp.dot(ya_ref[...], w_ref[0:gw, :], preferred_element_type=F32)
         + jnp.dot(yb_ref[...], w_ref[gw:2 * gw, :], preferred_element_type=F32)
         + jnp.dot(yc_ref[...], w_ref[2 * gw:3 * gw, :], preferred_element_type=F32)
         + jnp.dot(yd_ref[...], w_ref[3 * gw:, :], preferred_element_type=F32))
    is_ctx = _ctx_rows(pl.program_id(1), tm, ctx_len)
    _residual_epilogue(y, x_ref, mc_ref, mb_ref, g_ref, is_ctx, x_out_ref, h_out_ref)


def _out_proj(ya, yb, yc, yd, w, x, mc3, mb3, g2, *, nb, tb, ctx_len):
    d = x.shape[1]
    gw = GROUP_W
    tm = _pick_tile(tb, 768)
    nt = tb // tm
    row = lambda b, t: (b * nt + t, 0)
    c2 = lambda b, t: (0, 0)
    return pl.pallas_call(
        functools.partial(_out_proj_kernel, tm=tm, ctx_len=ctx_len),
        grid=(nb, nt),
        in_specs=[pl.BlockSpec((tm, gw), row)] * 4
        + [pl.BlockSpec((4 * gw, d), c2, pipeline_mode=pl.Buffered(1)), pl.BlockSpec((tm, d), row),
           pl.BlockSpec((3, d), c2), pl.BlockSpec((1, 3, d), lambda b, t: (b, 0, 0)), pl.BlockSpec((2, d), c2)],
        out_specs=[pl.BlockSpec((tm, d), row), pl.BlockSpec((tm, d), row)],
        out_shape=[jax.ShapeDtypeStruct(x.shape, F32), jax.ShapeDtypeStruct(x.shape, BF16)],
        compiler_params=_cparams("arbitrary", "arbitrary"),
    )(ya, yb, yc, yd, w, x, mc3, mb3, g2)


FFN_PAD = 8


FFN_ROW_CHUNK = 768


def _ffn_up_kernel(h_ref, wg_ref, wv_ref, cg_ref, cv_ref, o_ref, ug_ref, uv_ref, *, ctx_len):
    n, tn = o_ref.shape
    rc, pad = FFN_ROW_CHUNK, FFN_PAD
    nch = n // rc
    wg = wg_ref[...].astype(BF16)
    wv = wv_ref[...].astype(BF16)
    for ref in (ug_ref, uv_ref):
        ref[0:pad, :] = jnp.zeros((pad, tn), F32)
        ref[pad + n:2 * pad + n, :] = jnp.zeros((pad, tn), F32)
    local = lax.broadcasted_iota(jnp.int32, (rc, 1), 0)

    def matmuls(c):
        h = h_ref[c * rc:(c + 1) * rc, :]
        ug_ref[pad + c * rc:pad + (c + 1) * rc, :] = jnp.dot(h, wg, preferred_element_type=F32)
        uv_ref[pad + c * rc:pad + (c + 1) * rc, :] = jnp.dot(h, wv, preferred_element_type=F32)

    def conv(ref, w_ref, c):
        r0 = c * rc
        prev = ref[pad + r0 - 1:pad + r0 - 1 + rc, :]
        cur = ref[pad + r0:pad + r0 + rc, :]
        nxt = ref[pad + r0 + 1:pad + r0 + 1 + rc, :]
        if r0 <= ctx_len < r0 + rc:
            prev = jnp.where(local == ctx_len - r0, 0.0, prev)
        if r0 <= ctx_len - 1 < r0 + rc:
            nxt = jnp.where(local == ctx_len - 1 - r0, 0.0, nxt)
        return prev * w_ref[0:1, :] + cur * w_ref[1:2, :] + nxt * w_ref[2:3, :]

    def epilogue(c):
        gate = conv(ug_ref, cg_ref, c)
        val = conv(uv_ref, cv_ref, c)
        o_ref[c * rc:(c + 1) * rc, :] = (_silu(gate) * val).astype(o_ref.dtype)

    for c in range(nch + 2):
        if c < nch:
            matmuls(c)
        if c >= 2:
            epilogue(c - 2)


def _ffn_up(h, w_up, w_conv, layer, *, nb, tb, ctx_len):
    r_tot, d = h.shape
    dff = w_up.shape[2] // 2
    tn = 512
    nj = dff // tn
    assert tb % FFN_ROW_CHUNK == 0
    buf = pltpu.VMEM((tb + 2 * FFN_PAD, tn), F32)
    return pl.pallas_call(
        functools.partial(_ffn_up_kernel, ctx_len=ctx_len),
        grid=(nb, nj),
        in_specs=[pl.BlockSpec((tb, d), lambda b, j: (b, 0), pipeline_mode=pl.Buffered(1)),
                  pl.BlockSpec((None, d, tn), lambda b, j: (layer, 0, j)),
                  pl.BlockSpec((None, d, tn), lambda b, j: (layer, 0, nj + j)),
                  pl.BlockSpec((None, 3, tn), lambda b, j: (layer, 0, j)),
                  pl.BlockSpec((None, 3, tn), lambda b, j: (layer, 0, nj + j))],
        out_specs=pl.BlockSpec((tb, tn), lambda b, j: (b, j)),
        out_shape=jax.ShapeDtypeStruct((r_tot, dff), BF16),
        scratch_shapes=[buf, buf],
        compiler_params=_cparams("arbitrary", "arbitrary"),
    )(h, w_up, w_up, w_conv, w_conv)


def _ffn_down_kernel(a_ref, w_ref, x_ref, mc_ref, mb_ref, g_ref, x_out_ref, h_out_ref, *, tm, ctx_len):
    y = jnp.dot(a_ref[...], w_ref[...], preferred_element_type=F32)
    is_ctx = _ctx_rows(pl.program_id(1), tm, ctx_len)
    _residual_epilogue(y, x_ref, mc_ref, mb_ref, g_ref, is_ctx, x_out_ref, h_out_ref)


def _ffn_down(act, w, x, mc3, mb3, g2, *, nb, tb, ctx_len):
    d = x.shape[1]
    dff = act.shape[1]
    tm = _pick_tile(tb, 384)
    nt = tb // tm
    row = lambda b, t: (b * nt + t, 0)
    c2 = lambda b, t: (0, 0)
    return pl.pallas_call(
        functools.partial(_ffn_down_kernel, tm=tm, ctx_len=ctx_len),
        grid=(nb, nt),
        in_specs=[pl.BlockSpec((tm, dff), row),
                  pl.BlockSpec((dff, d), c2, pipeline_mode=pl.Buffered(1)),
                  pl.BlockSpec((tm, d), row),
                  pl.BlockSpec((3, d), c2), pl.BlockSpec((1, 3, d), lambda b, t: (b, 0, 0)), pl.BlockSpec((2, d), c2)],
        out_specs=[pl.BlockSpec((tm, d), row), pl.BlockSpec((tm, d), row)],
        out_shape=[jax.ShapeDtypeStruct(x.shape, F32), jax.ShapeDtypeStruct(x.shape, BF16)],
        compiler_params=_cparams("arbitrary", "arbitrary"),
    )(act, w, x, mc3, mb3, g2)


def _pad_cols(w, n):
    return jnp.pad(w, ((0, 0), (0, n - w.shape[1])))


def _layout_w_in(w_in):
    gw = GROUP_W
    a = w_in[:, :RWKV_IN]
    o = 3 * gw
    wa = jnp.concatenate([a[:, :o],
                          _pad_cols(a[:, o:o + RWKV_DECAY_LORA], RWKV_LORA_PAD),
                          _pad_cols(a[:, o + RWKV_DECAY_LORA:o + RWKV_DECAY_LORA + RWKV_ICLR_LORA], RWKV_LORA_PAD),
                          a[:, o + RWKV_DECAY_LORA + RWKV_ICLR_LORA:]], axis=1)
    wb = _pad_cols(w_in[:, RWKV_IN:RWKV_IN + MLA_IN], MLA_IN_PAD)
    wc = w_in[:, RWKV_IN + MLA_IN:RWKV_IN + MLA_IN + CONV_IN]
    wd = w_in[:, RWKV_IN + MLA_IN + CONV_IN:]
    w = jnp.concatenate([wa, wd, wc[:, :gw], wb, wc[:, gw:]], axis=1)
    assert w.shape[1] == P_WIDTH
    return w.astype(BF16)


def _layout_rwkv_shift(shift):
    o = 3 * GROUP_W
    return jnp.concatenate([shift[:, :o],
                            _pad_cols(shift[:, o:o + RWKV_DECAY_LORA], RWKV_LORA_PAD),
                            _pad_cols(shift[:, o + RWKV_DECAY_LORA:o + RWKV_DECAY_LORA + RWKV_ICLR_LORA], RWKV_LORA_PAD),
                            shift[:, o + RWKV_DECAY_LORA + RWKV_ICLR_LORA:]], axis=1)


def _pad_lora_rows(w):
    return jnp.pad(w, ((0, 0), (0, RWKV_LORA_PAD - w.shape[1]), (0, 0))).astype(BF16)


def _layout_mla(w_uq, w_ukv):
    hp = MLA_HEAD_PAD
    dqk = MLA_NOPE + MLA_ROPE
    wq = jnp.concatenate([_pad_cols(w_uq[:, h * dqk:(h + 1) * dqk], hp) for h in range(MLA_HEADS)], axis=1)
    dkv = MLA_NOPE + MLA_V
    wk = jnp.concatenate([_pad_cols(w_ukv[:, h * dkv:h * dkv + MLA_NOPE], hp) for h in range(MLA_HEADS)], axis=1)
    wv = jnp.concatenate([w_ukv[:, h * dkv + MLA_NOPE:(h + 1) * dkv] for h in range(MLA_HEADS)], axis=1)
    return wq.astype(BF16), wk.astype(BF16), wv.astype(BF16)


def _rope_tables(seq_len, ctx_len, rot_dim, lead, width):
    rows = seq_len // GRID_W
    half = rot_dim // 2
    quarter = half // 2
    inv = ROPE_BASE ** (-jnp.arange(0, half, 2, dtype=F32) / half)
    row = jnp.repeat(jnp.arange(rows, dtype=F32), GRID_W)
    col = jnp.tile(jnp.arange(GRID_W, dtype=F32), rows)
    zeros = jnp.zeros((seq_len, quarter), F32)
    cos_parts, sa_parts, sb_parts = [], [], []
    for pos in (row, col):
        ang = pos[:, None] * inv[None, :]
        c, s = jnp.cos(ang), jnp.sin(ang)
        cos_parts += [c, c]
        sa_parts += [-s, zeros]
        sb_parts += [zeros, s]

    def table(parts, fill):
        body = jnp.concatenate(parts, axis=1)
        body = jnp.concatenate([jnp.full((seq_len, lead), fill, F32), body,
                                jnp.full((seq_len, width - lead - rot_dim), fill, F32)], axis=1)
        return jnp.concatenate([jnp.full((ctx_len, width), fill, F32), body], axis=0)

    return table(cos_parts, 1.0), table(sa_parts, 0.0), table(sb_parts, 0.0)


def _head_sum_matrix(width, head):
    i = jnp.arange(width) // head
    return (i[:, None] == i[None, :]).astype(BF16)


def _chunk_tri(tm, chunk):
    i = jnp.arange(tm)
    same = (i[:, None] // chunk) == (i[None, :] // chunk)
    lower = jnp.logical_and(same, i[None, :] <= i[:, None])
    upper = jnp.logical_and(same, i[None, :] >= i[:, None])
    return jnp.stack([lower, upper]).astype(BF16)


def _token_mixer(h, lw, tables, *, nb, tb, ctx_len):
    kw = dict(nb=nb, tb=tb)
    p = _matmul(h, lw["w_in"], BF16, tn=640, **kw)
    pa = pb = pc = pd = p

    at, rt, bt, kt, wcum, v, gate, bonus = _rwkv_features(
        pa, lw["rwkv_shift"], lw["rwkv_w0"], lw["rwkv_w_up"], lw["rwkv_a0"], lw["rwkv_a_up"], lw["rwkv_g_up"],
        lw["rwkv_vecs"], tables["ones_bd"], tables["tri"], ctx_len=ctx_len, **kw)
    yf, yb_ = _rwkv_scan(at, rt, bt, kt, wcum, v, tables["scan_masks"], tables["same_head"],
                         ctx_len=ctx_len, feat_tm=256, **kw)
    ya = _rwkv_finish(yf, yb_, gate, bonus, lw["rwkv_vecs"], tables["ones_bd"], **kw)

    wq, wk, wv = lw["mla_w"]
    q, k, vv = _mla_project(pb, lw["mla_q_norm"], lw["mla_kv_norm"], wq, wk, wv, *tables["mla_rope"], **kw)
    yb = _mla_attention(q, k, vv, ctx_len=ctx_len, **kw)

    yc = _conv_mixer(pc, lw["conv_w"], ctx_len=ctx_len, **kw)
    yd = _retention(pd, lw["ret_decay"], lw["ret_gn_g"], *tables["ret_rope"], ctx_len=ctx_len, **kw)
    return ya, yb, yc, yd


def kernel(x, c, ctx, c_ctx, mod_w, mod_b, norm_g, w_in, rwkv_shift, rwkv_w0, rwkv_w_up, rwkv_a0, rwkv_a_up,
           rwkv_g_up, rwkv_vecs, mla_q_norm, mla_kv_norm, mla_w_uq, mla_w_ukv, conv_w, ret_decay, ret_gn_g, w_out,
           mlp_w_up, mlp_conv, mlp_w_down):
    nb, seq_len, d = x.shape
    ctx_len = ctx.shape[1]
    depth = mod_w.shape[0]
    tb = ctx_len + seq_len
    assert nb + 1 <= 8 and d == D_MODEL
    kw = dict(nb=nb, tb=tb, ctx_len=ctx_len)

    tables = {
        "ones_bd": _head_sum_matrix(GROUP_W, RWKV_HEAD),
        "tri": _chunk_tri(256, RWKV_CHUNK),
        "scan_masks": _rwkv_scan_masks(),
        "same_head": _head_sum_matrix(RWKV_QUAD, RWKV_HEAD),
        "mla_rope": _rope_tables(seq_len, ctx_len, MLA_ROPE, MLA_NOPE, MLA_HEAD_PAD),
        "ret_rope": _rope_tables(seq_len, ctx_len, RET_HEAD, 0, RET_HEAD),
    }

    c_pad = jnp.concatenate([c, c_ctx[None, :], jnp.zeros((8 - nb - 1, d), F32)], axis=0)
    mods = _modulation(c_pad, mod_w, mod_b).reshape(depth, 8, N_MOD, d)
    m_lat = mods[:, :nb]
    m_ctx = mods[:, nb]

    xs, h = _prologue(ctx, x, m_ctx[0], m_lat[0], norm_g[0, 0:1], **kw)

    for l in range(depth):
        lw = {
            "w_in": _layout_w_in(w_in[l]),
            "rwkv_shift": _layout_rwkv_shift(rwkv_shift[l]),
            "rwkv_w0": rwkv_w0[l], "rwkv_w_up": _pad_lora_rows(rwkv_w_up[l]),
            "rwkv_a0": rwkv_a0[l], "rwkv_a_up": _pad_lora_rows(rwkv_a_up[l]),
            "rwkv_g_up": rwkv_g_up[l].astype(BF16), "rwkv_vecs": rwkv_vecs[l],
            "mla_q_norm": mla_q_norm[l][None, :], "mla_kv_norm": mla_kv_norm[l][None, :],
            "mla_w": _layout_mla(mla_w_uq[l], mla_w_ukv[l]),
            "conv_w": conv_w[l],
            "ret_decay": jnp.broadcast_to(
                jnp.pad(ret_decay[l].T, ((0, 0), (0, 6)))[:, :, None], (RET_HEADS, 8, 128)),
            "ret_gn_g": ret_gn_g[l][None, :],
        }
        ya, yb, yc, yd = _token_mixer(h, lw, tables, **kw)

        xs, h = _out_proj(ya, yb, yc, yd, w_out[l].astype(BF16), xs,
                          m_ctx[l][jnp.array([2, 3, 4])], m_lat[l][:, jnp.array([2, 3, 4])],
                          norm_g[l, 1:3], **kw)
        act = _ffn_up(h, mlp_w_up, mlp_conv, l, **kw)
        nl = min(l + 1, depth - 1)
        mc3 = jnp.stack([m_ctx[l][5], m_ctx[nl][0], m_ctx[nl][1]])
        mb3 = jnp.stack([m_lat[l][:, 5], m_lat[nl][:, 0], m_lat[nl][:, 1]], axis=1)
        g2 = jnp.stack([norm_g[l, 3], norm_g[nl, 0]])
        xs, h = _ffn_down(act, mlp_w_down[l].astype(BF16), xs, mc3, mb3, g2, **kw)

    return xs.reshape(nb, tb, d)[:, ctx_len:]
```

```python
import functools
import math

import jax
import jax.numpy as jnp
from jax import lax
from jax.experimental import pallas as pl
from jax.experimental.pallas import tpu as pltpu

F32 = jnp.float32
BF16 = jnp.bfloat16

D_MODEL = 2048
GRID_W = 64
GROUP_W = 512
N_MOD = 6
NORM_EPS = 1e-6
ROPE_BASE = 10000.0

RWKV_HEAD = 64
RWKV_HEADS = GROUP_W // RWKV_HEAD
RWKV_DECAY_LORA = 96
RWKV_ICLR_LORA = 96
RWKV_GATE_LORA = 256
RWKV_IN = 3 * GROUP_W + RWKV_DECAY_LORA + RWKV_ICLR_LORA + RWKV_GATE_LORA
RWKV_LORA_PAD = 128
RWKV_IN_PAD = 3 * GROUP_W + 2 * RWKV_LORA_PAD + RWKV_GATE_LORA
RWKV_DECAY_SCALE = math.exp(-0.5)
RWKV_GN_EPS = 64e-5
RWKV_CHUNK = 64

MLA_HEADS = 4
MLA_NOPE = 128
MLA_ROPE = 64
MLA_V = 128
MLA_Q_RANK = 384
MLA_KV_RANK = 256
MLA_IN = MLA_Q_RANK + MLA_KV_RANK + MLA_ROPE
MLA_IN_PAD = 768
MLA_HEAD_PAD = 256
MLA_SCALE = (MLA_NOPE + MLA_ROPE) ** -0.5

CONV_IN = 3 * GROUP_W

P_RWKV = 0
P_RET = RWKV_IN_PAD
P_CONV_B = P_RET + 4 * GROUP_W
P_MLA = P_CONV_B + GROUP_W
P_CONV_C = P_MLA + MLA_IN_PAD
P_CONV_U = P_CONV_C + GROUP_W
P_WIDTH = P_CONV_U + GROUP_W

RET_HEADS = 4
RET_HEAD = 128
RET_IN = 4 * GROUP_W
GN_EPS = 1e-5

D_FF = 5632
VMEM_LIMIT = 56 * 1024 * 1024


def _cparams(*sem):
    return pltpu.CompilerParams(dimension_semantics=sem, vmem_limit_bytes=VMEM_LIMIT)


def _pick_tile(n, target, mult=16):
    best = None
    for t in range(mult, min(n, target) + 1, mult):
        if n % t == 0:
            best = t
    assert best is not None, (n, target)
    return best


def _dot(a, b):
    return jnp.dot(a.astype(BF16), b.astype(BF16), preferred_element_type=F32)


def _dot_nt(a, b):
    return lax.dot_general(a.astype(BF16), b.astype(BF16), (((1,), (1,)), ((), ())),
                           preferred_element_type=F32)


def _dot_tn(a, b):
    return lax.dot_general(a.astype(BF16), b.astype(BF16), (((0,), (0,)), ((), ())),
                           preferred_element_type=F32)


def _split3(x):
    hi = x.astype(BF16)
    r1 = x - hi.astype(F32)
    mid = r1.astype(BF16)
    lo = (r1 - mid.astype(F32)).astype(BF16)
    return hi, mid, lo


def _dot_wide_rhs(m, x):
    hi, mid, lo = _split3(x)
    return (jnp.dot(m, hi, preferred_element_type=F32) + jnp.dot(m, mid, preferred_element_type=F32)
            + jnp.dot(m, lo, preferred_element_type=F32))


def _dot_wide_lhs(x, m):
    hi, mid, lo = _split3(x)
    return (jnp.dot(hi, m, preferred_element_type=F32) + jnp.dot(mid, m, preferred_element_type=F32)
            + jnp.dot(lo, m, preferred_element_type=F32))


def _sigmoid(x):
    return 1.0 / (1.0 + jnp.exp(-x))


def _silu(x):
    return x * _sigmoid(x)


def _rms(x, g):
    ms = jnp.mean(x * x, axis=-1, keepdims=True)
    return x * lax.rsqrt(ms + NORM_EPS) * g


def _ctx_rows(tile_idx, tm, ctx_len):
    rows = tile_idx * tm + lax.broadcasted_iota(jnp.int32, (tm, 1), 0)
    return rows < ctx_len


def _mod_row(is_ctx, mc_ref, mb_ref, k):
    return jnp.where(is_ctx, mc_ref[k:k + 1, :], mb_ref[0, k:k + 1, :])


def _shift_rows(u, first_row, last_row):
    n = u.shape[0]
    rows = lax.broadcasted_iota(jnp.int32, (n, 1), 0)
    prev = jnp.where(rows == 0, first_row, pltpu.roll(u, 1, 0))
    nxt = jnp.where(rows == n - 1, last_row, pltpu.roll(u, n - 1, 0))
    return prev, nxt


def _rope(x, cos, sa, sb, half):
    n = x.shape[-1]
    return x * cos + pltpu.roll(x, n - half, 1) * sa + pltpu.roll(x, half, 1) * sb


def _mod_kernel(c_ref, w_ref, b_ref, o_ref):
    o_ref[0] = _dot(_silu(c_ref[...]), w_ref[0]) + b_ref[0]


def _modulation(c_pad, mod_w, mod_b):
    depth, d, n = mod_w.shape
    tn = 1024
    return pl.pallas_call(
        _mod_kernel,
        grid=(depth, n // tn),
        in_specs=[pl.BlockSpec((8, d), lambda l, j: (0, 0)),
                  pl.BlockSpec((1, d, tn), lambda l, j: (l, 0, j)),
                  pl.BlockSpec((1, 1, tn), lambda l, j: (l, 0, j))],
        out_specs=pl.BlockSpec((1, 8, tn), lambda l, j: (l, 0, j)),
        out_shape=jax.ShapeDtypeStruct((depth, 8, n), F32),
        compiler_params=_cparams("arbitrary", "arbitrary"),
    )(c_pad, mod_w, mod_b.reshape(depth, 1, n))


def _prologue_kernel(ctx_ref, x_ref, mc_ref, mb_ref, g_ref, xs_ref, h_ref, *, tm, ctx_len):
    t = pl.program_id(1)
    is_ctx = _ctx_rows(t, tm, ctx_len)
    x = jnp.where(t * tm < ctx_len, ctx_ref[...], x_ref[...])
    xs_ref[...] = x
    h = _rms(x, g_ref[...]) * (1.0 + _mod_row(is_ctx, mc_ref, mb_ref, 1)) + _mod_row(is_ctx, mc_ref, mb_ref, 0)
    h_ref[...] = h.astype(BF16)


def _prologue(ctx, x, mc, mb, g, *, nb, tb, ctx_len):
    d = x.shape[2]
    tm = 256
    assert ctx_len % tm == 0 and tb % tm == 0
    nt = tb // tm
    ct = ctx_len // tm
    row = lambda b, t: (b * nt + t, 0)
    return pl.pallas_call(
        functools.partial(_prologue_kernel, tm=tm, ctx_len=ctx_len),
        grid=(nb, nt),
        in_specs=[pl.BlockSpec((None, tm, d), lambda b, t: (b, jnp.minimum(t, ct - 1), 0)),
                  pl.BlockSpec((None, tm, d), lambda b, t: (b, jnp.maximum(t - ct, 0), 0)),
                  pl.BlockSpec((N_MOD, d), lambda b, t: (0, 0)),
                  pl.BlockSpec((1, N_MOD, d), lambda b, t: (b, 0, 0)),
                  pl.BlockSpec((1, d), lambda b, t: (0, 0))],
        out_specs=[pl.BlockSpec((tm, d), row), pl.BlockSpec((tm, d), row)],
        out_shape=[jax.ShapeDtypeStruct((nb * tb, d), F32), jax.ShapeDtypeStruct((nb * tb, d), BF16)],
        compiler_params=_cparams("arbitrary", "arbitrary"),
    )(ctx, x, mc, mb, g)


def _matmul_kernel(a_ref, w_ref, o_ref):
    o_ref[...] = jnp.dot(a_ref[...], w_ref[...], preferred_element_type=F32).astype(o_ref.dtype)


def _matmul(a, w, out_dtype, *, nb, tb, tn):
    k, n = w.shape
    assert n % tn == 0
    return pl.pallas_call(
        _matmul_kernel,
        grid=(nb, n // tn),
        in_specs=[pl.BlockSpec((tb, k), lambda b, j: (b, 0)),
                  pl.BlockSpec((k, tn), lambda b, j: (0, j))],
        out_specs=pl.BlockSpec((tb, tn), lambda b, j: (b, j)),
        out_shape=jax.ShapeDtypeStruct((a.shape[0], n), out_dtype),
        compiler_params=_cparams("arbitrary", "arbitrary"),
    )(a, w)


HALO_ROWS = 16


def _rwkv_feat_kernel(p_ref, pprev_ref, pnext_ref, shift_ref, w0_ref, wup_ref, a0_ref, aup_ref, gup_ref,
                      vecs_ref, ones_ref, tri_ref,
                      at_ref, rt_ref, bt_ref, kt_ref, wc_ref, v_ref, gate_ref, bonus_ref,
                      *, tm, ctx_tiles, nt):
    t = pl.program_id(1)
    first = jnp.logical_or(t == 0, t == ctx_tiles)
    last = jnp.logical_or(t == ctx_tiles - 1, t == nt - 1)
    u = p_ref[...].astype(F32)
    halo_prev = jnp.where(first, 0.0, pprev_ref[HALO_ROWS - 1:HALO_ROWS, :].astype(F32))
    halo_next = jnp.where(last, 0.0, pnext_ref[0:1, :].astype(F32))
    prev, nxt = _shift_rows(u, halo_prev, halo_next)
    p = u + shift_ref[0:1, :] * (prev - u) + shift_ref[1:2, :] * (nxt - u)

    gw = GROUP_W
    r = p[:, 0:gw]
    k = p[:, gw:2 * gw]
    v = p[:, 2 * gw:3 * gw]
    wd = p[:, 3 * gw:3 * gw + RWKV_LORA_PAD]
    ad = p[:, 3 * gw + RWKV_LORA_PAD:3 * gw + 2 * RWKV_LORA_PAD]
    gd = p[:, 3 * gw + 2 * RWKV_LORA_PAD:]
    k_k = vecs_ref[0:1, :]
    k_a = vecs_ref[1:2, :]
    r_k = vecs_ref[2:3, :]
    ones = ones_ref[...]

    kk = k * k_k
    kk = kk * lax.rsqrt(_dot_wide_lhs(kk * kk, ones) + 1e-12)
    v_ref[...] = v.astype(BF16)
    gate_ref[...] = _dot(_sigmoid(gd), gup_ref[...])
    bonus_ref[...] = _dot_wide_lhs(r * k * r_k, ones) * v

    tanh_wd = jnp.tanh(wd)
    nch = tm // RWKV_CHUNK
    for d in range(2):
        lw = -RWKV_DECAY_SCALE * _sigmoid(w0_ref[d:d + 1, :] + _dot(tanh_wd, wup_ref[d]))
        l_inc = _dot_wide_rhs(tri_ref[d], lw)
        l_exc = l_inc - lw
        asig = _sigmoid(a0_ref[d:d + 1, :] + _dot(ad, aup_ref[d]))
        k_d = k * (1.0 + (asig - 1.0) * k_a)
        e_inc = jnp.exp(l_inc)
        e_neg = jnp.exp(-l_inc)
        at_ref[d] = (-kk * jnp.exp(l_exc)).astype(BF16)
        rt_ref[d] = (r * e_inc).astype(BF16)
        bt_ref[d] = (kk * asig * e_neg).astype(BF16)
        kt_ref[d] = (k_d * e_neg).astype(BF16)
        end = RWKV_CHUNK - 1 if d == 0 else 0
        rows = [e_inc[c * RWKV_CHUNK + end:c * RWKV_CHUNK + end + 1, :] for c in range(nch)]
        rows.append(jnp.zeros((8 - nch, gw), F32))
        wc_ref[d] = jnp.concatenate(rows, axis=0)


def _rwkv_features(pa, shift, w0, wup, a0, aup, gup, vecs, ones_bd, tri, *, nb, tb, ctx_len):
    r_tot = pa.shape[0]
    tm = 256
    assert ctx_len % tm == 0 and tb % tm == 0
    nt = tb // tm
    hb = tm // HALO_ROWS
    nhalo = r_tot // HALO_ROWS
    gw = GROUP_W
    row = lambda b, t: (b * nt + t, 0)
    drow = lambda b, t: (0, b * nt + t, 0)
    const2 = lambda b, t: (0, 0)
    const3 = lambda b, t: (0, 0, 0)
    feat = jax.ShapeDtypeStruct((2, r_tot, gw), BF16)
    return pl.pallas_call(
        functools.partial(_rwkv_feat_kernel, tm=tm, ctx_tiles=ctx_len // tm, nt=nt),
        grid=(nb, nt),
        in_specs=[pl.BlockSpec((tm, RWKV_IN_PAD), row),
                  pl.BlockSpec((HALO_ROWS, RWKV_IN_PAD), lambda b, t: (jnp.maximum((b * nt + t) * hb - 1, 0), 0)),
                  pl.BlockSpec((HALO_ROWS, RWKV_IN_PAD), lambda b, t: (jnp.minimum((b * nt + t + 1) * hb, nhalo - 1), 0)),
                  pl.BlockSpec((2, RWKV_IN_PAD), const2),
                  pl.BlockSpec((2, gw), const2),
                  pl.BlockSpec((2, RWKV_LORA_PAD, gw), const3),
                  pl.BlockSpec((2, gw), const2),
                  pl.BlockSpec((2, RWKV_LORA_PAD, gw), const3),
                  pl.BlockSpec((RWKV_GATE_LORA, gw), const2),
                  pl.BlockSpec((5, gw), const2),
                  pl.BlockSpec((gw, gw), const2),
                  pl.BlockSpec((2, tm, tm), const3)],
        out_specs=[pl.BlockSpec((2, tm, gw), drow)] * 4
        + [pl.BlockSpec((2, 8, gw), drow),
           pl.BlockSpec((tm, gw), row), pl.BlockSpec((tm, gw), row), pl.BlockSpec((tm, gw), row)],
        out_shape=[feat, feat, feat, feat,
                   jax.ShapeDtypeStruct((2, r_tot // tm * 8, gw), F32),
                   jax.ShapeDtypeStruct((r_tot, gw), BF16),
                   jax.ShapeDtypeStruct((r_tot, gw), F32),
                   jax.ShapeDtypeStruct((r_tot, gw), F32)],
        compiler_params=_cparams("arbitrary", "arbitrary"),
    )(pa, pa, pa, shift, w0, wup, a0, aup, gup, vecs, ones_bd, tri)


RWKV_QUAD = 4 * RWKV_HEAD
RWKV_STEP_CHUNKS = 4
RWKV_INV_LEVELS = 6
M_STRICT, M_INCL, M_LEVEL0, M_EYE, M_SAME = 0, 1, 2, 2 + RWKV_INV_LEVELS, 3 + RWKV_INV_LEVELS
RWKV_N_MASKS = 4 + RWKV_INV_LEVELS


def _rwkv_scan_masks():
    n, c = RWKV_QUAD, RWKV_CHUNK
    r = jnp.arange(n)[:, None]
    col = jnp.arange(n)[None, :]
    same = (r // c) == (col // c)
    t, j = r % c, col % c
    out = []
    for d in range(2):
        before = (j < t) if d == 0 else (j > t)
        ms = [same & before, same & (before | (j == t))]
        for lvl in range(RWKV_INV_LEVELS):
            s = 2 ** lvl
            blk = (r // (2 * s)) == (col // (2 * s))
            late_r = (r % (2 * s)) >= s
            late_c = (col % (2 * s)) >= s
            ms.append(blk & ((late_r & ~late_c) if d == 0 else (~late_r & late_c)))
        ms += [r == col, same]
        out.append(jnp.stack(ms))
    return jnp.stack(out).astype(F32)


def _head_stack(x):
    hd = RWKV_HEAD
    return jnp.concatenate([x[:, h * hd:(h + 1) * hd] for h in range(4)], axis=0)


def _head_unstack(x):
    c = RWKV_CHUNK
    return jnp.concatenate([x[h * c:(h + 1) * c, :] for h in range(4)], axis=1)


def _rwkv_scan_kernel(atf_ref, rtf_ref, btf_ref, ktf_ref, vf_ref, wcf_ref,
                      atb_ref, rtb_ref, btb_ref, ktb_ref, vb_ref, wcb_ref, mask_ref, same_ref,
                      yf_ref, yb_ref, s_ref, *, n_groups, ctx_groups, per_tile):
    i = pl.program_id(1)

    @pl.when(i == 0)
    def _():
        s_ref[...] = jnp.zeros_like(s_ref)

    nsub = RWKV_STEP_CHUNKS
    c_len = RWKV_CHUNK
    g_bwd = jnp.where(i < ctx_groups, ctx_groups - 1 - i, n_groups - 1 + ctx_groups - i)
    first_chunk = (i * nsub, g_bwd * nsub)
    ins = ((atf_ref, rtf_ref, btf_ref, ktf_ref, vf_ref, wcf_ref, yf_ref),
           (atb_ref, rtb_ref, btb_ref, ktb_ref, vb_ref, wcb_ref, yb_ref))
    qw = RWKV_QUAD
    n4 = 4 * c_len
    chains = [(d, q) for d in range(2) for q in range(GROUP_W // qw)]
    order = (list(range(nsub)), list(range(nsub - 1, -1, -1)))
    work = [(d, q, sub) for sub in range(nsub) for (d, q) in chains]

    def masked_stack(x):
        return jnp.concatenate([x, x, x, x], axis=0) * same_ref[...]

    def block_diag(x_st, d, m):
        return jnp.concatenate([x_st, x_st, x_st, x_st], axis=1) * mask_ref[d, m]

    st = {}
    for w in work:
        d, q, sub = w
        a_ref, r_ref, b_ref, k_ref, v_ref, _, _ = ins[d]
        sl = slice(q * qw, (q + 1) * qw)
        rows = slice(sub * c_len, (sub + 1) * c_len)
        b_, k_ = b_ref[0, rows, sl], k_ref[0, rows, sl]
        st[w] = dict(ms_ar=jnp.concatenate([masked_stack(a_ref[0, rows, sl]), masked_stack(r_ref[0, rows, sl])], axis=0),
                     ms_b=masked_stack(b_), ms_k=masked_stack(k_), bk=jnp.concatenate([b_, k_], axis=0),
                     v_st=_head_stack(v_ref[rows, sl]))
    for w in work:
        e = st[w]
        e["aa"] = _dot_nt(e["ms_ar"], e["bk"])
    for w in work:
        d = w[0]
        e = st[w]
        e["a_ab"] = block_diag(e["aa"][:n4, :c_len], d, M_STRICT)
        e["a_ak"] = block_diag(e["aa"][:n4, c_len:], d, M_STRICT)
        e["a_rb"] = block_diag(e["aa"][n4:, :c_len], d, M_INCL)
        e["a_rk"] = block_diag(e["aa"][n4:, c_len:], d, M_INCL)
        e["t"] = mask_ref[d, M_EYE] + e["a_ab"] * mask_ref[d, M_LEVEL0]
    for lvl in range(1, RWKV_INV_LEVELS):
        for w in work:
            e = st[w]
            e["w"] = _dot(e["a_ab"] * mask_ref[w[0], M_LEVEL0 + lvl], e["t"])
        for w in work:
            e = st[w]
            e["t"] = e["t"] + _dot(e["t"], e["w"])
    for w in work:
        e = st[w]
        e["akv"] = _dot(e["a_ak"], e["v_st"])
        e["rkv"] = _dot(e["a_rk"], e["v_st"])
        e["vtk"] = _dot_tn(e["v_st"], e["ms_k"])
    state = {ch: s_ref[ch[0], :, ch[1] * qw:(ch[1] + 1) * qw] for ch in chains}
    for step in range(nsub):
        cur = [(d, q, order[d][step]) for (d, q) in chains]
        for w in cur:
            e = st[w]
            e["xs"] = _dot_nt(e["ms_ar"], state[w[:2]])
        for w in cur:
            e = st[w]
            e["u"] = _dot(e["t"], e["xs"][:n4] + e["akv"])
        for w in cur:
            d, q, sub = w
            e = st[w]
            sl = slice(q * qw, (q + 1) * qw)
            y_st = e["xs"][n4:] + _dot(e["a_rb"], e["u"]) + e["rkv"]
            ins[d][6][sub * c_len:(sub + 1) * c_len, sl] = _head_unstack(y_st)
            wc = ins[d][5][0, pl.ds((first_chunk[d] + sub) % per_tile, 1), sl]
            state[(d, q)] = (state[(d, q)] + _dot_tn(e["u"], e["ms_b"]) + e["vtk"]) * wc
    for (d, q) in chains:
        s_ref[d, :, q * qw:(q + 1) * qw] = state[(d, q)]


def _rwkv_scan(at, rt, bt, kt, wc, v, masks, same, *, nb, tb, ctx_len, feat_tm):
    r_tot = v.shape[0]
    gw = GROUP_W
    nsub = RWKV_STEP_CHUNKS
    c = nsub * RWKV_CHUNK
    per_tile = feat_tm // RWKV_CHUNK
    assert tb % c == 0 and ctx_len % c == 0 and per_tile % nsub == 0
    n_groups = tb // c
    ctx_groups = ctx_len // c

    def cf(b, i):
        return b * n_groups + i

    def cb(b, i):
        return b * n_groups + jnp.where(i < ctx_groups, ctx_groups - 1 - i, n_groups - 1 + ctx_groups - i)

    def specs(d, cidx):
        feat = pl.BlockSpec((1, c, gw), lambda b, i: (d, cidx(b, i), 0))
        return [feat, feat, feat, feat,
                pl.BlockSpec((c, gw), lambda b, i: (cidx(b, i), 0)),
                pl.BlockSpec((1, 8, gw), lambda b, i: (d, cidx(b, i) * nsub // per_tile, 0))]

    return pl.pallas_call(
        functools.partial(_rwkv_scan_kernel, n_groups=n_groups, ctx_groups=ctx_groups, per_tile=per_tile),
        grid=(nb, n_groups),
        in_specs=specs(0, cf) + specs(1, cb)
        + [pl.BlockSpec((2, RWKV_N_MASKS, RWKV_QUAD, RWKV_QUAD), lambda b, i: (0, 0, 0, 0)),
           pl.BlockSpec((RWKV_QUAD, RWKV_QUAD), lambda b, i: (0, 0))],
        out_specs=[pl.BlockSpec((c, gw), lambda b, i: (cf(b, i), 0)),
                   pl.BlockSpec((c, gw), lambda b, i: (cb(b, i), 0))],
        out_shape=[jax.ShapeDtypeStruct((r_tot, gw), F32), jax.ShapeDtypeStruct((r_tot, gw), F32)],
        scratch_shapes=[pltpu.VMEM((2, RWKV_HEAD, gw), F32)],
        compiler_params=_cparams("arbitrary", "arbitrary"),
    )(at, rt, bt, kt, v, wc, at, rt, bt, kt, v, wc, masks, same)


def _rwkv_finish_kernel(yf_ref, yb_ref, gate_ref, bonus_ref, vecs_ref, ones_ref, o_ref):
    ones = ones_ref[...]
    y = yf_ref[...] + yb_ref[...]
    inv = 1.0 / RWKV_HEAD
    mu = _dot_wide_lhs(y, ones) * inv
    yc = y - mu
    var = _dot_wide_lhs(yc * yc, ones) * inv
    out = yc * lax.rsqrt(var + RWKV_GN_EPS) * vecs_ref[3:4, :] + vecs_ref[4:5, :] + bonus_ref[...]
    o_ref[...] = (out * gate_ref[...]).astype(o_ref.dtype)


def _rwkv_finish(yf, yb, gate, bonus, vecs, ones_bd, *, nb, tb):
    r_tot = gate.shape[0]
    gw = GROUP_W
    tm = _pick_tile(tb, 768)
    nt = tb // tm
    row = lambda b, t: (b * nt + t, 0)
    return pl.pallas_call(
        _rwkv_finish_kernel,
        grid=(nb, nt),
        in_specs=[pl.BlockSpec((tm, gw), row), pl.BlockSpec((tm, gw), row),
                  pl.BlockSpec((tm, gw), row), pl.BlockSpec((tm, gw), row),
                  pl.BlockSpec((5, gw), lambda b, t: (0, 0)),
                  pl.BlockSpec((gw, gw), lambda b, t: (0, 0))],
        out_specs=pl.BlockSpec((tm, gw), row),
        out_shape=jax.ShapeDtypeStruct((r_tot, gw), BF16),
        compiler_params=_cparams("arbitrary", "arbitrary"),
    )(yf, yb, gate, bonus, vecs, ones_bd)


def _mla_proj_kernel(p_ref, qn_ref, kvn_ref, wq_ref, wk_ref, wv_ref, cos_ref, sa_ref, sb_ref,
                     q_ref, k_ref, v_ref):
    p = p_ref[...].astype(F32)
    c_q = p[:, :MLA_Q_RANK]
    c_kv = p[:, MLA_Q_RANK:MLA_Q_RANK + MLA_KV_RANK]
    k_r = p[:, MLA_Q_RANK + MLA_KV_RANK:]
    n_q = _rms(c_q, qn_ref[...])
    n_kv = _rms(c_kv, kvn_ref[...])
    q = _dot(n_q, wq_ref[...])
    kx = _dot(n_kv, wk_ref[...])
    v_ref[...] = _dot(n_kv, wv_ref[...]).astype(BF16)
    cos, sa, sb = cos_ref[...], sa_ref[...], sb_ref[...]
    kr_blk = jnp.concatenate([jnp.zeros_like(k_r), k_r], axis=1)
    hp = MLA_HEAD_PAD
    for h in range(MLA_HEADS):
        sl = slice(h * hp, (h + 1) * hp)
        q_ref[:, sl] = _rope(q[:, sl], cos, sa, sb, MLA_ROPE // 4).astype(BF16)
        k_ref[:, sl] = _rope(kx[:, sl] + kr_blk, cos, sa, sb, MLA_ROPE // 4).astype(BF16)


def _mla_project(pb, qn, kvn, wq, wk, wv, cos, sa, sb, *, nb, tb):
    r_tot = pb.shape[0]
    tm = _pick_tile(tb, 768)
    nt = tb // tm
    row = lambda b, t: (b * nt + t, 0)
    trow = lambda b, t: (t, 0)
    c2 = lambda b, t: (0, 0)
    hw = MLA_HEADS * MLA_HEAD_PAD
    return pl.pallas_call(
        _mla_proj_kernel,
        grid=(nb, nt),
        in_specs=[pl.BlockSpec((tm, MLA_IN_PAD), lambda b, t: (b * nt + t, P_MLA // MLA_IN_PAD)),
                  pl.BlockSpec((1, MLA_Q_RANK), c2), pl.BlockSpec((1, MLA_KV_RANK), c2),
                  pl.BlockSpec((MLA_Q_RANK, hw), c2), pl.BlockSpec((MLA_KV_RANK, hw), c2),
                  pl.BlockSpec((MLA_KV_RANK, GROUP_W), c2),
                  pl.BlockSpec((tm, MLA_HEAD_PAD), trow), pl.BlockSpec((tm, MLA_HEAD_PAD), trow),
                  pl.BlockSpec((tm, MLA_HEAD_PAD), trow)],
        out_specs=[pl.BlockSpec((tm, hw), row), pl.BlockSpec((tm, hw), row), pl.BlockSpec((tm, GROUP_W), row)],
        out_shape=[jax.ShapeDtypeStruct((r_tot, hw), BF16), jax.ShapeDtypeStruct((r_tot, hw), BF16),
                   jax.ShapeDtypeStruct((r_tot, GROUP_W), BF16)],
        compiler_params=_cparams("arbitrary", "arbitrary"),
    )(pb, qn, kvn, wq, wk, wv, cos, sa, sb)


def _softmax_pv(s, v):
    s = s * MLA_SCALE
    m = jnp.max(s, axis=-1, keepdims=True)
    e = jnp.exp(s - m)
    l = jnp.sum(e, axis=-1, keepdims=True)
    return _dot(e, v) / l


def _mla_attn_kernel(q_ref, k_ref, v_ref, o_ref, *, ctx_len, tq, n_qt):
    k_all = k_ref[...]
    v_all = v_ref[...]
    o_ref[0:ctx_len, :] = _softmax_pv(_dot_nt(q_ref[0:ctx_len, :], k_all[0:ctx_len]), v_all[0:ctx_len]).astype(BF16)

    for i in range(n_qt):
        rows = slice(ctx_len + i * tq, ctx_len + (i + 1) * tq)
        o_ref[rows, :] = _softmax_pv(_dot_nt(q_ref[rows, :], k_all), v_all).astype(BF16)


def _mla_attention(q, k, v, *, nb, tb, ctx_len):
    r_tot = q.shape[0]
    tq = 256
    assert (tb - ctx_len) % tq == 0 and ctx_len % tq == 0
    hp = MLA_HEAD_PAD
    return pl.pallas_call(
        functools.partial(_mla_attn_kernel, ctx_len=ctx_len, tq=tq, n_qt=(tb - ctx_len) // tq),
        grid=(nb, MLA_HEADS),
        in_specs=[pl.BlockSpec((tb, hp), lambda b, h: (b, h)),
                  pl.BlockSpec((tb, hp), lambda b, h: (b, h)),
                  pl.BlockSpec((tb, MLA_V), lambda b, h: (b, h))],
        out_specs=pl.BlockSpec((tb, MLA_V), lambda b, h: (b, h)),
        out_shape=jax.ShapeDtypeStruct((r_tot, GROUP_W), BF16),
        compiler_params=_cparams("arbitrary", "arbitrary"),
    )(q, k, v)


def _seq_edges(n, ctx_len):
    rows = lax.broadcasted_iota(jnp.int32, (n, 1), 0)
    starts = jnp.logical_or(rows == 0, rows == ctx_len)
    ends = jnp.logical_or(rows == ctx_len - 1, rows == n - 1)
    return starts, ends


def _dwconv3_rows(u, w_ref, starts, ends):
    n = u.shape[0]
    prev = jnp.where(starts, 0.0, pltpu.roll(u, 1, 0))
    nxt = jnp.where(ends, 0.0, pltpu.roll(u, n - 1, 0))
    return prev * w_ref[0:1, :] + u * w_ref[1:2, :] + nxt * w_ref[2:3, :]


def _conv_mix_kernel(b_ref, c_ref, u_ref, w_ref, o_ref, *, ctx_len):
    starts, ends = _seq_edges(b_ref.shape[0], ctx_len)
    z = c_ref[...].astype(F32) * u_ref[...].astype(F32)
    o_ref[...] = (b_ref[...].astype(F32) * _dwconv3_rows(z, w_ref, starts, ends)).astype(o_ref.dtype)


def _conv_mixer(pc, conv_w, *, nb, tb, ctx_len):
    r_tot = pc.shape[0]
    tc = 256
    nj = GROUP_W // tc
    return pl.pallas_call(
        functools.partial(_conv_mix_kernel, ctx_len=ctx_len),
        grid=(nb, nj),
        in_specs=[pl.BlockSpec((tb, tc), lambda b, j: (b, P_CONV_B // tc + j)),
                  pl.BlockSpec((tb, tc), lambda b, j: (b, P_CONV_C // tc + j)),
                  pl.BlockSpec((tb, tc), lambda b, j: (b, P_CONV_U // tc + j)),
                  pl.BlockSpec((3, tc), lambda b, j: (0, j))],
        out_specs=pl.BlockSpec((tb, tc), lambda b, j: (b, j)),
        out_shape=jax.ShapeDtypeStruct((r_tot, GROUP_W), BF16),
        compiler_params=_cparams("arbitrary", "arbitrary"),
    )(pc, pc, pc, conv_w)


def _ret_kernel(q_ref, k_ref, v_ref, g_ref, dec_ref, gn_ref, cos_ref, sa_ref, sb_ref, o_ref,
                qs_ref, ks_ref, vs_ref, dm_ref, *, ctx_len, seq_len):
    blk = ctx_len
    nlat = seq_len // blk
    tb = ctx_len + seq_len
    cos, sa, sb = cos_ref[...], sa_ref[...], sb_ref[...]
    qs_ref[...] = _rope(q_ref[...].astype(F32), cos, sa, sb, RET_HEAD // 4).astype(BF16)
    kr = _rope(k_ref[...].astype(F32), cos, sa, sb, RET_HEAD // 4).astype(BF16)
    ks_ref[0:tb, :] = kr
    ks_ref[tb:tb + blk, :] = kr[0:blk]
    vs_ref[0:tb, :] = v_ref[...]
    vs_ref[tb:tb + blk, :] = v_ref[0:blk, :]
    lgf = -jnp.exp(dec_ref[0, 0:1, 0:1])
    lgb = -jnp.exp(dec_ref[0, 1:2, 0:1])
    gn = gn_ref[...]
    scale = RET_HEAD ** -0.5
    rc = (lax.broadcasted_iota(jnp.int32, (blk, blk), 0) - lax.broadcasted_iota(jnp.int32, (blk, blk), 1)).astype(F32)
    dm_ref[0] = jnp.exp(jnp.where(rc >= 0, rc * lgf, -rc * lgb))
    for dist in range(1, nlat + 1):
        dm_ref[dist] = jnp.exp((dist * blk + rc) * lgf)
        dm_ref[nlat + dist] = jnp.exp((dist * blk - rc) * lgb)

    def finish(o, g):
        mu = jnp.mean(o, axis=-1, keepdims=True)
        oc = o - mu
        var = jnp.mean(oc * oc, axis=-1, keepdims=True)
        return _silu(g) * (oc * lax.rsqrt(var + GN_EPS) * gn)

    s_c = _dot_nt(qs_ref[0:blk, :], ks_ref[0:blk, :]) * dm_ref[0]
    o_c = _dot(s_c, vs_ref[0:blk, :]) * scale
    o_ref[0:blk, :] = finish(o_c, g_ref[0:blk, :].astype(F32)).astype(o_ref.dtype)

    for i in range(nlat):
        rows = slice(ctx_len + i * blk, ctx_len + (i + 1) * blk)
        s = _dot_nt(qs_ref[rows, :], ks_ref[...])
        parts = []
        for jp in range(nlat + 2):
            j = jp - 1
            m = dm_ref[i - j] if j < i else (dm_ref[0] if j == i else dm_ref[nlat + j - i])
            parts.append((s[:, jp * blk:(jp + 1) * blk] * m).astype(BF16))
        o = jnp.dot(jnp.concatenate(parts, axis=1), vs_ref[...], preferred_element_type=F32) * scale
        o_ref[rows, :] = finish(o, g_ref[rows, :].astype(F32)).astype(o_ref.dtype)


def _retention(pd, dec, gn, cos, sa, sb, *, nb, tb, ctx_len):
    r_tot = pd.shape[0]
    hd = RET_HEAD
    nh = RET_HEADS
    seq_len = tb - ctx_len
    assert seq_len % ctx_len == 0
    nlat = seq_len // ctx_len
    tbl = pl.BlockSpec((tb, hd), lambda b, h: (0, 0))
    return pl.pallas_call(
        functools.partial(_ret_kernel, ctx_len=ctx_len, seq_len=seq_len),
        grid=(nb, nh),
        in_specs=[pl.BlockSpec((tb, hd), lambda b, h: (b, P_RET // hd + h)),
                  pl.BlockSpec((tb, hd), lambda b, h: (b, P_RET // hd + nh + h)),
                  pl.BlockSpec((tb, hd), lambda b, h: (b, P_RET // hd + 2 * nh + h)),
                  pl.BlockSpec((tb, hd), lambda b, h: (b, P_RET // hd + 3 * nh + h)),
                  pl.BlockSpec((1, 8, 128), lambda b, h: (h, 0, 0)),
                  pl.BlockSpec((1, hd), lambda b, h: (0, h)),
                  tbl, tbl, tbl],
        out_specs=pl.BlockSpec((tb, hd), lambda b, h: (b, h)),
        out_shape=jax.ShapeDtypeStruct((r_tot, GROUP_W), BF16),
        scratch_shapes=[pltpu.VMEM((tb, hd), BF16), pltpu.VMEM((tb + ctx_len, hd), BF16),
                        pltpu.VMEM((tb + ctx_len, hd), BF16), pltpu.VMEM((2 * nlat + 1, ctx_len, ctx_len), F32)],
        compiler_params=_cparams("arbitrary", "arbitrary"),
    )(pd, pd, pd, pd, dec, gn, cos, sa, sb)


def _residual_epilogue(y, x_ref, mc_ref, mb_ref, g_ref, is_ctx, x_out_ref, h_out_ref):
    x_new = x_ref[...] + _mod_row(is_ctx, mc_ref, mb_ref, 0) * _rms(y, g_ref[0:1, :])
    x_out_ref[...] = x_new
    h = _rms(x_new, g_ref[1:2, :]) * (1.0 + _mod_row(is_ctx, mc_ref, mb_ref, 2)) + _mod_row(is_ctx, mc_ref, mb_ref, 1)
    h_out_ref[...] = h.astype(BF16)


def _out_proj_kernel(ya_ref, yb_ref, yc_ref, yd_ref, w_ref, x_ref, mc_ref, mb_ref, g_ref, x_out_ref, h_out_ref,
                     *, tm, ctx_len):
    gw = GROUP_W
    y = (jnp.dot(ya_ref[...], w_ref[0:gw, :], preferred_element_type=F32)
         + jnp.dot(yb_ref[...], w_ref[gw:2 * gw, :], preferred_element_type=F32)
         + jnp.dot(yc_ref[...], w_ref[2 * gw:3 * gw, :], preferred_element_type=F32)
         + jnp.dot(yd_ref[...], w_ref[3 * gw:, :], preferred_element_type=F32))
    is_ctx = _ctx_rows(pl.program_id(1), tm, ctx_len)
    _residual_epilogue(y, x_ref, mc_ref, mb_ref, g_ref, is_ctx, x_out_ref, h_out_ref)


def _out_proj(ya, yb, yc, yd, w, x, mc3, mb3, g2, *, nb, tb, ctx_len):
    d = x.shape[1]
    gw = GROUP_W
    tm = _pick_tile(tb, 768)
    nt = tb // tm
    row = lambda b, t: (b * nt + t, 0)
    c2 = lambda b, t: (0, 0)
    return pl.pallas_call(
        functools.partial(_out_proj_kernel, tm=tm, ctx_len=ctx_len),
        grid=(nb, nt),
        in_specs=[pl.BlockSpec((tm, gw), row)] * 4
        + [pl.BlockSpec((4 * gw, d), c2, pipeline_mode=pl.Buffered(1)), pl.BlockSpec((tm, d), row),
           pl.BlockSpec((3, d), c2), pl.BlockSpec((1, 3, d), lambda b, t: (b, 0, 0)), pl.BlockSpec((2, d), c2)],
        out_specs=[pl.BlockSpec((tm, d), row), pl.BlockSpec((tm, d), row)],
        out_shape=[jax.ShapeDtypeStruct(x.shape, F32), jax.ShapeDtypeStruct(x.shape, BF16)],
        compiler_params=_cparams("arbitrary", "arbitrary"),
    )(ya, yb, yc, yd, w, x, mc3, mb3, g2)


FFN_PAD = 8


FFN_ROW_CHUNK = 768


def _ffn_up_kernel(h_ref, wg_ref, wv_ref, cg_ref, cv_ref, o_ref, ug_ref, uv_ref, *, ctx_len):
    n, tn = o_ref.shape
    rc, pad = FFN_ROW_CHUNK, FFN_PAD
    nch = n // rc
    wg = wg_ref[...].astype(BF16)
    wv = wv_ref[...].astype(BF16)
    for ref in (ug_ref, uv_ref):
        ref[0:pad, :] = jnp.zeros((pad, tn), F32)
        ref[pad + n:2 * pad + n, :] = jnp.zeros((pad, tn), F32)
    local = lax.broadcasted_iota(jnp.int32, (rc, 1), 0)

    def matmuls(c):
        h = h_ref[c * rc:(c + 1) * rc, :]
        ug_ref[pad + c * rc:pad + (c + 1) * rc, :] = jnp.dot(h, wg, preferred_element_type=F32)
        uv_ref[pad + c * rc:pad + (c + 1) * rc, :] = jnp.dot(h, wv, preferred_element_type=F32)

    def conv(ref, w_ref, c):
        r0 = c * rc
        prev = ref[pad + r0 - 1:pad + r0 - 1 + rc, :]
        cur = ref[pad + r0:pad + r0 + rc, :]
        nxt = ref[pad + r0 + 1:pad + r0 + 1 + rc, :]
        if r0 <= ctx_len < r0 + rc:
            prev = jnp.where(local == ctx_len - r0, 0.0, prev)
        if r0 <= ctx_len - 1 < r0 + rc:
            nxt = jnp.where(local == ctx_len - 1 - r0, 0.0, nxt)
        return prev * w_ref[0:1, :] + cur * w_ref[1:2, :] + nxt * w_ref[2:3, :]

    def epilogue(c):
        gate = conv(ug_ref, cg_ref, c)
        val = conv(uv_ref, cv_ref, c)
        o_ref[c * rc:(c + 1) * rc, :] = (_silu(gate) * val).astype(o_ref.dtype)

    for c in range(nch + 2):
        if c < nch:
            matmuls(c)
        if c >= 2:
            epilogue(c - 2)


def _ffn_up(h, w_up, w_conv, layer, *, nb, tb, ctx_len):
    r_tot, d = h.shape
    dff = w_up.shape[2] // 2
    tn = 512
    nj = dff // tn
    assert tb % FFN_ROW_CHUNK == 0
    buf = pltpu.VMEM((tb + 2 * FFN_PAD, tn), F32)
    return pl.pallas_call(
        functools.partial(_ffn_up_kernel, ctx_len=ctx_len),
        grid=(nb, nj),
        in_specs=[pl.BlockSpec((tb, d), lambda b, j: (b, 0), pipeline_mode=pl.Buffered(1)),
                  pl.BlockSpec((None, d, tn), lambda b, j: (layer, 0, j)),
                  pl.BlockSpec((None, d, tn), lambda b, j: (layer, 0, nj + j)),
                  pl.BlockSpec((None, 3, tn), lambda b, j: (layer, 0, j)),
                  pl.BlockSpec((None, 3, tn), lambda b, j: (layer, 0, nj + j))],
        out_specs=pl.BlockSpec((tb, tn), lambda b, j: (b, j)),
        out_shape=jax.ShapeDtypeStruct((r_tot, dff), BF16),
        scratch_shapes=[buf, buf],
        compiler_params=_cparams("arbitrary", "arbitrary"),
    )(h, w_up, w_up, w_conv, w_conv)


def _ffn_down_kernel(a_ref, w_ref, x_ref, mc_ref, mb_ref, g_ref, x_out_ref, h_out_ref, *, tm, ctx_len):
    y = jnp.dot(a_ref[...], w_ref[...], preferred_element_type=F32)
    is_ctx = _ctx_rows(pl.program_id(1), tm, ctx_len)
    _residual_epilogue(y, x_ref, mc_ref, mb_ref, g_ref, is_ctx, x_out_ref, h_out_ref)


def _ffn_down(act, w, x, mc3, mb3, g2, *, nb, tb, ctx_len):
    d = x.shape[1]
    dff = act.shape[1]
    tm = _pick_tile(tb, 384)
    nt = tb // tm
    row = lambda b, t: (b * nt + t, 0)
    c2 = lambda b, t: (0, 0)
    return pl.pallas_call(
        functools.partial(_ffn_down_kernel, tm=tm, ctx_len=ctx_len),
        grid=(nb, nt),
        in_specs=[pl.BlockSpec((tm, dff), row),
                  pl.BlockSpec((dff, d), c2, pipeline_mode=pl.Buffered(1)),
                  pl.BlockSpec((tm, d), row),
                  pl.BlockSpec((3, d), c2), pl.BlockSpec((1, 3, d), lambda b, t: (b, 0, 0)), pl.BlockSpec((2, d), c2)],
        out_specs=[pl.BlockSpec((tm, d), row), pl.BlockSpec((tm, d), row)],
        out_shape=[jax.ShapeDtypeStruct(x.shape, F32), jax.ShapeDtypeStruct(x.shape, BF16)],
        compiler_params=_cparams("arbitrary", "arbitrary"),
    )(act, w, x, mc3, mb3, g2)


def _pad_cols(w, n):
    return jnp.pad(w, ((0, 0), (0, n - w.shape[1])))


def _layout_w_in(w_in):
    gw = GROUP_W
    a = w_in[:, :RWKV_IN]
    o = 3 * gw
    wa = jnp.concatenate([a[:, :o],
                          _pad_cols(a[:, o:o + RWKV_DECAY_LORA], RWKV_LORA_PAD),
                          _pad_cols(a[:, o + RWKV_DECAY_LORA:o + RWKV_DECAY_LORA + RWKV_ICLR_LORA], RWKV_LORA_PAD),
                          a[:, o + RWKV_DECAY_LORA + RWKV_ICLR_LORA:]], axis=1)
    wb = _pad_cols(w_in[:, RWKV_IN:RWKV_IN + MLA_IN], MLA_IN_PAD)
    wc = w_in[:, RWKV_IN + MLA_IN:RWKV_IN + MLA_IN + CONV_IN]
    wd = w_in[:, RWKV_IN + MLA_IN + CONV_IN:]
    w = jnp.concatenate([wa, wd, wc[:, :gw], wb, wc[:, gw:]], axis=1)
    assert w.shape[1] == P_WIDTH
    return w.astype(BF16)


def _layout_rwkv_shift(shift):
    o = 3 * GROUP_W
    return jnp.concatenate([shift[:, :o],
                            _pad_cols(shift[:, o:o + RWKV_DECAY_LORA], RWKV_LORA_PAD),
                            _pad_cols(shift[:, o + RWKV_DECAY_LORA:o + RWKV_DECAY_LORA + RWKV_ICLR_LORA], RWKV_LORA_PAD),
                            shift[:, o + RWKV_DECAY_LORA + RWKV_ICLR_LORA:]], axis=1)


def _pad_lora_rows(w):
    return jnp.pad(w, ((0, 0), (0, RWKV_LORA_PAD - w.shape[1]), (0, 0))).astype(BF16)


def _layout_mla(w_uq, w_ukv):
    hp = MLA_HEAD_PAD
    dqk = MLA_NOPE + MLA_ROPE
    wq = jnp.concatenate([_pad_cols(w_uq[:, h * dqk:(h + 1) * dqk], hp) for h in range(MLA_HEADS)], axis=1)
    dkv = MLA_NOPE + MLA_V
    wk = jnp.concatenate([_pad_cols(w_ukv[:, h * dkv:h * dkv + MLA_NOPE], hp) for h in range(MLA_HEADS)], axis=1)
    wv = jnp.concatenate([w_ukv[:, h * dkv + MLA_NOPE:(h + 1) * dkv] for h in range(MLA_HEADS)], axis=1)
    return wq.astype(BF16), wk.astype(BF16), wv.astype(BF16)


def _rope_tables(seq_len, ctx_len, rot_dim, lead, width):
    rows = seq_len // GRID_W
    half = rot_dim // 2
    quarter = half // 2
    inv = ROPE_BASE ** (-jnp.arange(0, half, 2, dtype=F32) / half)
    row = jnp.repeat(jnp.arange(rows, dtype=F32), GRID_W)
    col = jnp.tile(jnp.arange(GRID_W, dtype=F32), rows)
    zeros = jnp.zeros((seq_len, quarter), F32)
    cos_parts, sa_parts, sb_parts = [], [], []
    for pos in (row, col):
        ang = pos[:, None] * inv[None, :]
        c, s = jnp.cos(ang), jnp.sin(ang)
        cos_parts += [c, c]
        sa_parts += [-s, zeros]
        sb_parts += [zeros, s]

    def table(parts, fill):
        body = jnp.concatenate(parts, axis=1)
        body = jnp.concatenate([jnp.full((seq_len, lead), fill, F32), body,
                                jnp.full((seq_len, width - lead - rot_dim), fill, F32)], axis=1)
        return jnp.concatenate([jnp.full((ctx_len, width), fill, F32), body], axis=0)

    return table(cos_parts, 1.0), table(sa_parts, 0.0), table(sb_parts, 0.0)


def _head_sum_matrix(width, head):
    i = jnp.arange(width) // head
    return (i[:, None] == i[None, :]).astype(BF16)


def _chunk_tri(tm, chunk):
    i = jnp.arange(tm)
    same = (i[:, None] // chunk) == (i[None, :] // chunk)
    lower = jnp.logical_and(same, i[None, :] <= i[:, None])
    upper = jnp.logical_and(same, i[None, :] >= i[:, None])
    return jnp.stack([lower, upper]).astype(BF16)


def _token_mixer(h, lw, tables, *, nb, tb, ctx_len):
    kw = dict(nb=nb, tb=tb)
    p = _matmul(h, lw["w_in"], BF16, tn=1280, **kw)
    pa = pb = pc = pd = p

    at, rt, bt, kt, wcum, v, gate, bonus = _rwkv_features(
        pa, lw["rwkv_shift"], lw["rwkv_w0"], lw["rwkv_w_up"], lw["rwkv_a0"], lw["rwkv_a_up"], lw["rwkv_g_up"],
        lw["rwkv_vecs"], tables["ones_bd"], tables["tri"], ctx_len=ctx_len, **kw)
    yf, yb_ = _rwkv_scan(at, rt, bt, kt, wcum, v, tables["scan_masks"], tables["same_head"],
                         ctx_len=ctx_len, feat_tm=256, **kw)
    ya = _rwkv_finish(yf, yb_, gate, bonus, lw["rwkv_vecs"], tables["ones_bd"], **kw)

    wq, wk, wv = lw["mla_w"]
    q, k, vv = _mla_project(pb, lw["mla_q_norm"], lw["mla_kv_norm"], wq, wk, wv, *tables["mla_rope"], **kw)
    yb = _mla_attention(q, k, vv, ctx_len=ctx_len, **kw)

    yc = _conv_mixer(pc, lw["conv_w"], ctx_len=ctx_len, **kw)
    yd = _retention(pd, lw["ret_decay"], lw["ret_gn_g"], *tables["ret_rope"], ctx_len=ctx_len, **kw)
    return ya, yb, yc, yd


def kernel(x, c, ctx, c_ctx, mod_w, mod_b, norm_g, w_in, rwkv_shift, rwkv_w0, rwkv_w_up, rwkv_a0, rwkv_a_up,
           rwkv_g_up, rwkv_vecs, mla_q_norm, mla_kv_norm, mla_w_uq, mla_w_ukv, conv_w, ret_decay, ret_gn_g, w_out,
           mlp_w_up, mlp_conv, mlp_w_down):
    nb, seq_len, d = x.shape
    ctx_len = ctx.shape[1]
    depth = mod_w.shape[0]
    tb = ctx_len + seq_len
    assert nb + 1 <= 8 and d == D_MODEL
    kw = dict(nb=nb, tb=tb, ctx_len=ctx_len)

    tables = {
        "ones_bd": _head_sum_matrix(GROUP_W, RWKV_HEAD),
        "tri": _chunk_tri(256, RWKV_CHUNK),
        "scan_masks": _rwkv_scan_masks(),
        "same_head": _head_sum_matrix(RWKV_QUAD, RWKV_HEAD),
        "mla_rope": _rope_tables(seq_len, ctx_len, MLA_ROPE, MLA_NOPE, MLA_HEAD_PAD),
        "ret_rope": _rope_tables(seq_len, ctx_len, RET_HEAD, 0, RET_HEAD),
    }

    c_pad = jnp.concatenate([c, c_ctx[None, :], jnp.zeros((8 - nb - 1, d), F32)], axis=0)
    mods = _modulation(c_pad, mod_w, mod_b).reshape(depth, 8, N_MOD, d)
    m_lat = mods[:, :nb]
    m_ctx = mods[:, nb]

    xs, h = _prologue(ctx, x, m_ctx[0], m_lat[0], norm_g[0, 0:1], **kw)

    for l in range(depth):
        lw = {
            "w_in": _layout_w_in(w_in[l]),
            "rwkv_shift": _layout_rwkv_shift(rwkv_shift[l]),
            "rwkv_w0": rwkv_w0[l], "rwkv_w_up": _pad_lora_rows(rwkv_w_up[l]),
            "rwkv_a0": rwkv_a0[l], "rwkv_a_up": _pad_lora_rows(rwkv_a_up[l]),
            "rwkv_g_up": rwkv_g_up[l].astype(BF16), "rwkv_vecs": rwkv_vecs[l],
            "mla_q_norm": mla_q_norm[l][None, :], "mla_kv_norm": mla_kv_norm[l][None, :],
            "mla_w": _layout_mla(mla_w_uq[l], mla_w_ukv[l]),
            "conv_w": conv_w[l],
            "ret_decay": jnp.broadcast_to(
                jnp.pad(ret_decay[l].T, ((0, 0), (0, 6)))[:, :, None], (RET_HEADS, 8, 128)),
            "ret_gn_g": ret_gn_g[l][None, :],
        }
        ya, yb, yc, yd = _token_mixer(h, lw, tables, **kw)

        xs, h = _out_proj(ya, yb, yc, yd, w_out[l].astype(BF16), xs,
                          m_ctx[l][jnp.array([2, 3, 4])], m_lat[l][:, jnp.array([2, 3, 4])],
                          norm_g[l, 1:3], **kw)
        act = _ffn_up(h, mlp_w_up, mlp_conv, l, **kw)
        nl = min(l + 1, depth - 1)
        mc3 = jnp.stack([m_ctx[l][5], m_ctx[nl][0], m_ctx[nl][1]])
        mb3 = jnp.stack([m_lat[l][:, 5], m_lat[nl][:, 0], m_lat[nl][:, 1]], axis=1)
        g2 = jnp.stack([norm_g[l, 3], norm_g[nl, 0]])
        xs, h = _ffn_down(act, mlp_w_down[l].astype(BF16), xs, mc3, mb3, g2, **kw)

    return xs.reshape(nb, tb, d)[:, ctx_len:]
```

```python
import functools
import math

import jax
import jax.numpy as jnp
from jax import lax
from jax.experimental import pallas as pl
from jax.experimental.pallas import tpu as pltpu

F32 = jnp.float32
BF16 = jnp.bfloat16

D_MODEL = 2048
GRID_W = 64
GROUP_W = 512
N_MOD = 6
NORM_EPS = 1e-6
ROPE_BASE = 10000.0

RWKV_HEAD = 64
RWKV_HEADS = GROUP_W // RWKV_HEAD
RWKV_DECAY_LORA = 96
RWKV_ICLR_LORA = 96
RWKV_GATE_LORA = 256
RWKV_IN = 3 * GROUP_W + RWKV_DECAY_LORA + RWKV_ICLR_LORA + RWKV_GATE_LORA
RWKV_LORA_PAD = 128
RWKV_IN_PAD = 3 * GROUP_W + 2 * RWKV_LORA_PAD + RWKV_GATE_LORA
RWKV_DECAY_SCALE = math.exp(-0.5)
RWKV_GN_EPS = 64e-5
RWKV_CHUNK = 64

MLA_HEADS = 4
MLA_NOPE = 128
MLA_ROPE = 64
MLA_V = 128
MLA_Q_RANK = 384
MLA_KV_RANK = 256
MLA_IN = MLA_Q_RANK + MLA_KV_RANK + MLA_ROPE
MLA_IN_PAD = 768
MLA_HEAD_PAD = 256
MLA_SCALE = (MLA_NOPE + MLA_ROPE) ** -0.5
MLA_LOGIT_SCALE = MLA_SCALE * math.log2(math.e)

CONV_IN = 3 * GROUP_W

P_RWKV = 0
P_RET = RWKV_IN_PAD
P_CONV_B = P_RET + 4 * GROUP_W
P_MLA = P_CONV_B + GROUP_W
P_CONV_C = P_MLA + MLA_IN_PAD
P_CONV_U = P_CONV_C + GROUP_W
P_WIDTH = P_CONV_U + GROUP_W

RET_HEADS = 4
RET_HEAD = 128
RET_IN = 4 * GROUP_W
GN_EPS = 1e-5

D_FF = 5632
VMEM_LIMIT = 56 * 1024 * 1024


def _cparams(*sem):
    return pltpu.CompilerParams(dimension_semantics=sem, vmem_limit_bytes=VMEM_LIMIT)


def _pick_tile(n, target, mult=16):
    best = None
    for t in range(mult, min(n, target) + 1, mult):
        if n % t == 0:
            best = t
    assert best is not None, (n, target)
    return best


def _dot(a, b):
    return jnp.dot(a.astype(BF16), b.astype(BF16), preferred_element_type=F32)


def _dot_nt(a, b):
    return lax.dot_general(a.astype(BF16), b.astype(BF16), (((1,), (1,)), ((), ())),
                           preferred_element_type=F32)


def _dot_tn(a, b):
    return lax.dot_general(a.astype(BF16), b.astype(BF16), (((0,), (0,)), ((), ())),
                           preferred_element_type=F32)


def _split3(x):
    hi = x.astype(BF16)
    r1 = x - hi.astype(F32)
    mid = r1.astype(BF16)
    lo = (r1 - mid.astype(F32)).astype(BF16)
    return hi, mid, lo


def _dot_wide_rhs(m, x):
    hi, mid, lo = _split3(x)
    return (jnp.dot(m, hi, preferred_element_type=F32) + jnp.dot(m, mid, preferred_element_type=F32)
            + jnp.dot(m, lo, preferred_element_type=F32))


def _dot_wide_lhs(x, m):
    hi, mid, lo = _split3(x)
    return (jnp.dot(hi, m, preferred_element_type=F32) + jnp.dot(mid, m, preferred_element_type=F32)
            + jnp.dot(lo, m, preferred_element_type=F32))


def _sigmoid(x):
    return 1.0 / (1.0 + jnp.exp(-x))


def _silu(x):
    return x * _sigmoid(x)


def _rms(x, g):
    ms = jnp.mean(x * x, axis=-1, keepdims=True)
    return x * lax.rsqrt(ms + NORM_EPS) * g


def _ctx_rows(tile_idx, tm, ctx_len):
    rows = tile_idx * tm + lax.broadcasted_iota(jnp.int32, (tm, 1), 0)
    return rows < ctx_len


def _mod_row(is_ctx, mc_ref, mb_ref, k):
    return jnp.where(is_ctx, mc_ref[k:k + 1, :], mb_ref[0, k:k + 1, :])


def _shift_rows(u, first_row, last_row):
    n = u.shape[0]
    rows = lax.broadcasted_iota(jnp.int32, (n, 1), 0)
    prev = jnp.where(rows == 0, first_row, pltpu.roll(u, 1, 0))
    nxt = jnp.where(rows == n - 1, last_row, pltpu.roll(u, n - 1, 0))
    return prev, nxt


def _rope(x, cos, sa, sb, half):
    n = x.shape[-1]
    return x * cos + pltpu.roll(x, n - half, 1) * sa + pltpu.roll(x, half, 1) * sb


def _mod_kernel(c_ref, w_ref, b_ref, o_ref):
    o_ref[0] = _dot(_silu(c_ref[...]), w_ref[0]) + b_ref[0]


def _modulation(c_pad, mod_w, mod_b):
    depth, d, n = mod_w.shape
    tn = 1024
    return pl.pallas_call(
        _mod_kernel,
        grid=(depth, n // tn),
        in_specs=[pl.BlockSpec((8, d), lambda l, j: (0, 0)),
                  pl.BlockSpec((1, d, tn), lambda l, j: (l, 0, j)),
                  pl.BlockSpec((1, 1, tn), lambda l, j: (l, 0, j))],
        out_specs=pl.BlockSpec((1, 8, tn), lambda l, j: (l, 0, j)),
        out_shape=jax.ShapeDtypeStruct((depth, 8, n), F32),
        compiler_params=_cparams("arbitrary", "arbitrary"),
    )(c_pad, mod_w, mod_b.reshape(depth, 1, n))


def _prologue_kernel(ctx_ref, x_ref, mc_ref, mb_ref, g_ref, xs_ref, h_ref, *, tm, ctx_len):
    t = pl.program_id(1)
    is_ctx = _ctx_rows(t, tm, ctx_len)
    x = jnp.where(t * tm < ctx_len, ctx_ref[...], x_ref[...])
    xs_ref[...] = x
    h = _rms(x, g_ref[...]) * (1.0 + _mod_row(is_ctx, mc_ref, mb_ref, 1)) + _mod_row(is_ctx, mc_ref, mb_ref, 0)
    h_ref[...] = h.astype(BF16)


def _prologue(ctx, x, mc, mb, g, *, nb, tb, ctx_len):
    d = x.shape[2]
    tm = 256
    assert ctx_len % tm == 0 and tb % tm == 0
    nt = tb // tm
    ct = ctx_len // tm
    row = lambda b, t: (b * nt + t, 0)
    return pl.pallas_call(
        functools.partial(_prologue_kernel, tm=tm, ctx_len=ctx_len),
        grid=(nb, nt),
        in_specs=[pl.BlockSpec((None, tm, d), lambda b, t: (b, jnp.minimum(t, ct - 1), 0)),
                  pl.BlockSpec((None, tm, d), lambda b, t: (b, jnp.maximum(t - ct, 0), 0)),
                  pl.BlockSpec((N_MOD, d), lambda b, t: (0, 0)),
                  pl.BlockSpec((1, N_MOD, d), lambda b, t: (b, 0, 0)),
                  pl.BlockSpec((1, d), lambda b, t: (0, 0))],
        out_specs=[pl.BlockSpec((tm, d), row), pl.BlockSpec((tm, d), row)],
        out_shape=[jax.ShapeDtypeStruct((nb * tb, d), F32), jax.ShapeDtypeStruct((nb * tb, d), BF16)],
        compiler_params=_cparams("arbitrary", "arbitrary"),
    )(ctx, x, mc, mb, g)


def _matmul_kernel(a_ref, w_ref, o_ref):
    o_ref[...] = jnp.dot(a_ref[...], w_ref[...], preferred_element_type=F32).astype(o_ref.dtype)


def _matmul(a, w, out_dtype, *, nb, tb, tn):
    k, n = w.shape
    assert n % tn == 0
    return pl.pallas_call(
        _matmul_kernel,
        grid=(nb, n // tn),
        in_specs=[pl.BlockSpec((tb, k), lambda b, j: (b, 0)),
                  pl.BlockSpec((k, tn), lambda b, j: (0, j))],
        out_specs=pl.BlockSpec((tb, tn), lambda b, j: (b, j)),
        out_shape=jax.ShapeDtypeStruct((a.shape[0], n), out_dtype),
        compiler_params=_cparams("arbitrary", "arbitrary"),
    )(a, w)


HALO_ROWS = 16


def _rwkv_feat_kernel(p_ref, pprev_ref, pnext_ref, shift_ref, w0_ref, wup_ref, a0_ref, aup_ref, gup_ref,
                      vecs_ref, ones_ref, tri_ref,
                      at_ref, rt_ref, bt_ref, kt_ref, wc_ref, v_ref, gate_ref, bonus_ref,
                      *, tm, ctx_tiles, nt):
    t = pl.program_id(1)
    first = jnp.logical_or(t == 0, t == ctx_tiles)
    last = jnp.logical_or(t == ctx_tiles - 1, t == nt - 1)
    u = p_ref[...].astype(F32)
    halo_prev = jnp.where(first, 0.0, pprev_ref[HALO_ROWS - 1:HALO_ROWS, :].astype(F32))
    halo_next = jnp.where(last, 0.0, pnext_ref[0:1, :].astype(F32))
    prev, nxt = _shift_rows(u, halo_prev, halo_next)
    p = u + shift_ref[0:1, :] * (prev - u) + shift_ref[1:2, :] * (nxt - u)

    gw = GROUP_W
    r = p[:, 0:gw]
    k = p[:, gw:2 * gw]
    v = p[:, 2 * gw:3 * gw]
    wd = p[:, 3 * gw:3 * gw + RWKV_LORA_PAD]
    ad = p[:, 3 * gw + RWKV_LORA_PAD:3 * gw + 2 * RWKV_LORA_PAD]
    gd = p[:, 3 * gw + 2 * RWKV_LORA_PAD:]
    k_k = vecs_ref[0:1, :]
    k_a = vecs_ref[1:2, :]
    r_k = vecs_ref[2:3, :]
    ones = ones_ref[...]

    kk = k * k_k
    kk = kk * lax.rsqrt(_dot_wide_lhs(kk * kk, ones) + 1e-12)
    v_ref[...] = v.astype(BF16)
    gate_ref[...] = _dot(_sigmoid(gd), gup_ref[...])
    bonus_ref[...] = _dot_wide_lhs(r * k * r_k, ones) * v

    tanh_wd = jnp.tanh(wd)
    nch = tm // RWKV_CHUNK
    for d in range(2):
        lw = -RWKV_DECAY_SCALE * _sigmoid(w0_ref[d:d + 1, :] + _dot(tanh_wd, wup_ref[d]))
        l_inc = _dot_wide_rhs(tri_ref[d], lw)
        l_exc = l_inc - lw
        asig = _sigmoid(a0_ref[d:d + 1, :] + _dot(ad, aup_ref[d]))
        k_d = k * (1.0 + (asig - 1.0) * k_a)
        e_inc = jnp.exp(l_inc)
        e_neg = jnp.exp(-l_inc)
        at_ref[d] = (-kk * jnp.exp(l_exc)).astype(BF16)
        rt_ref[d] = (r * e_inc).astype(BF16)
        bt_ref[d] = (kk * asig * e_neg).astype(BF16)
        kt_ref[d] = (k_d * e_neg).astype(BF16)
        end = RWKV_CHUNK - 1 if d == 0 else 0
        rows = [e_inc[c * RWKV_CHUNK + end:c * RWKV_CHUNK + end + 1, :] for c in range(nch)]
        rows.append(jnp.zeros((8 - nch, gw), F32))
        wc_ref[d] = jnp.concatenate(rows, axis=0)


def _rwkv_features(pa, shift, w0, wup, a0, aup, gup, vecs, ones_bd, tri, *, nb, tb, ctx_len):
    r_tot = pa.shape[0]
    tm = 256
    assert ctx_len % tm == 0 and tb % tm == 0
    nt = tb // tm
    hb = tm // HALO_ROWS
    nhalo = r_tot // HALO_ROWS
    gw = GROUP_W
    row = lambda b, t: (b * nt + t, 0)
    drow = lambda b, t: (0, b * nt + t, 0)
    const2 = lambda b, t: (0, 0)
    const3 = lambda b, t: (0, 0, 0)
    feat = jax.ShapeDtypeStruct((2, r_tot, gw), BF16)
    return pl.pallas_call(
        functools.partial(_rwkv_feat_kernel, tm=tm, ctx_tiles=ctx_len // tm, nt=nt),
        grid=(nb, nt),
        in_specs=[pl.BlockSpec((tm, RWKV_IN_PAD), row),
                  pl.BlockSpec((HALO_ROWS, RWKV_IN_PAD), lambda b, t: (jnp.maximum((b * nt + t) * hb - 1, 0), 0)),
                  pl.BlockSpec((HALO_ROWS, RWKV_IN_PAD), lambda b, t: (jnp.minimum((b * nt + t + 1) * hb, nhalo - 1), 0)),
                  pl.BlockSpec((2, RWKV_IN_PAD), const2),
                  pl.BlockSpec((2, gw), const2),
                  pl.BlockSpec((2, RWKV_LORA_PAD, gw), const3),
                  pl.BlockSpec((2, gw), const2),
                  pl.BlockSpec((2, RWKV_LORA_PAD, gw), const3),
                  pl.BlockSpec((RWKV_GATE_LORA, gw), const2),
                  pl.BlockSpec((5, gw), const2),
                  pl.BlockSpec((gw, gw), const2),
                  pl.BlockSpec((2, tm, tm), const3)],
        out_specs=[pl.BlockSpec((2, tm, gw), drow)] * 4
        + [pl.BlockSpec((2, 8, gw), drow),
           pl.BlockSpec((tm, gw), row), pl.BlockSpec((tm, gw), row), pl.BlockSpec((tm, gw), row)],
        out_shape=[feat, feat, feat, feat,
                   jax.ShapeDtypeStruct((2, r_tot // tm * 8, gw), F32),
                   jax.ShapeDtypeStruct((r_tot, gw), BF16),
                   jax.ShapeDtypeStruct((r_tot, gw), F32),
                   jax.ShapeDtypeStruct((r_tot, gw), F32)],
        compiler_params=_cparams("arbitrary", "arbitrary"),
    )(pa, pa, pa, shift, w0, wup, a0, aup, gup, vecs, ones_bd, tri)


RWKV_QUAD = 4 * RWKV_HEAD
RWKV_STEP_CHUNKS = 4
RWKV_INV_LEVELS = 6
M_STRICT, M_INCL, M_LEVEL0, M_EYE, M_SAME = 0, 1, 2, 2 + RWKV_INV_LEVELS, 3 + RWKV_INV_LEVELS
RWKV_N_MASKS = 4 + RWKV_INV_LEVELS


def _rwkv_scan_masks():
    n, c = RWKV_QUAD, RWKV_CHUNK
    r = jnp.arange(n)[:, None]
    col = jnp.arange(n)[None, :]
    same = (r // c) == (col // c)
    t, j = r % c, col % c
    out = []
    for d in range(2):
        before = (j < t) if d == 0 else (j > t)
        ms = [same & before, same & (before | (j == t))]
        for lvl in range(RWKV_INV_LEVELS):
            s = 2 ** lvl
            blk = (r // (2 * s)) == (col // (2 * s))
            late_r = (r % (2 * s)) >= s
            late_c = (col % (2 * s)) >= s
            ms.append(blk & ((late_r & ~late_c) if d == 0 else (~late_r & late_c)))
        ms += [r == col, same]
        out.append(jnp.stack(ms))
    return jnp.stack(out).astype(F32)


def _head_stack(x):
    hd = RWKV_HEAD
    return jnp.concatenate([x[:, h * hd:(h + 1) * hd] for h in range(4)], axis=0)


def _head_unstack(x):
    c = RWKV_CHUNK
    return jnp.concatenate([x[h * c:(h + 1) * c, :] for h in range(4)], axis=1)


def _rwkv_scan_kernel(atf_ref, rtf_ref, btf_ref, ktf_ref, vf_ref, wcf_ref,
                      atb_ref, rtb_ref, btb_ref, ktb_ref, vb_ref, wcb_ref, mask_ref, same_ref,
                      yf_ref, yb_ref, s_ref, *, n_groups, ctx_groups, per_tile):
    i = pl.program_id(1)

    @pl.when(i == 0)
    def _():
        s_ref[...] = jnp.zeros_like(s_ref)

    nsub = RWKV_STEP_CHUNKS
    c_len = RWKV_CHUNK
    g_bwd = jnp.where(i < ctx_groups, ctx_groups - 1 - i, n_groups - 1 + ctx_groups - i)
    first_chunk = (i * nsub, g_bwd * nsub)
    ins = ((atf_ref, rtf_ref, btf_ref, ktf_ref, vf_ref, wcf_ref, yf_ref),
           (atb_ref, rtb_ref, btb_ref, ktb_ref, vb_ref, wcb_ref, yb_ref))
    qw = RWKV_QUAD
    n4 = 4 * c_len
    chains = [(d, q) for d in range(2) for q in range(GROUP_W // qw)]
    order = (list(range(nsub)), list(range(nsub - 1, -1, -1)))
    work = [(d, q, sub) for sub in range(nsub) for (d, q) in chains]

    def masked_stack(x):
        return jnp.concatenate([x, x, x, x], axis=0) * same_ref[...]

    def block_diag(x_st, d, m):
        return jnp.concatenate([x_st, x_st, x_st, x_st], axis=1) * mask_ref[d, m]

    st = {}
    for w in work:
        d, q, sub = w
        a_ref, r_ref, b_ref, k_ref, v_ref, _, _ = ins[d]
        sl = slice(q * qw, (q + 1) * qw)
        rows = slice(sub * c_len, (sub + 1) * c_len)
        b_, k_ = b_ref[0, rows, sl], k_ref[0, rows, sl]
        st[w] = dict(ms_ar=jnp.concatenate([masked_stack(a_ref[0, rows, sl]), masked_stack(r_ref[0, rows, sl])], axis=0),
                     ms_b=masked_stack(b_), ms_k=masked_stack(k_), bk=jnp.concatenate([b_, k_], axis=0),
                     v_st=_head_stack(v_ref[rows, sl]))
    for w in work:
        e = st[w]
        e["aa"] = _dot_nt(e["ms_ar"], e["bk"])
    for w in work:
        d = w[0]
        e = st[w]
        e["a_ab"] = block_diag(e["aa"][:n4, :c_len], d, M_STRICT)
        e["a_ak"] = block_diag(e["aa"][:n4, c_len:], d, M_STRICT)
        e["a_rb"] = block_diag(e["aa"][n4:, :c_len], d, M_INCL)
        e["a_rk"] = block_diag(e["aa"][n4:, c_len:], d, M_INCL)
        e["t"] = mask_ref[d, M_EYE] + e["a_ab"] * mask_ref[d, M_LEVEL0]
    for lvl in range(1, RWKV_INV_LEVELS):
        for w in work:
            e = st[w]
            e["w"] = _dot(e["a_ab"] * mask_ref[w[0], M_LEVEL0 + lvl], e["t"])
        for w in work:
            e = st[w]
            e["t"] = e["t"] + _dot(e["t"], e["w"])
    for w in work:
        e = st[w]
        e["akv"] = _dot(e["a_ak"], e["v_st"])
        e["rkv"] = _dot(e["a_rk"], e["v_st"])
        e["vtk"] = _dot_tn(e["v_st"], e["ms_k"])
    state = {ch: s_ref[ch[0], :, ch[1] * qw:(ch[1] + 1) * qw] for ch in chains}
    for step in range(nsub):
        cur = [(d, q, order[d][step]) for (d, q) in chains]
        for w in cur:
            e = st[w]
            e["xs"] = _dot_nt(e["ms_ar"], state[w[:2]])
        for w in cur:
            e = st[w]
            e["u"] = _dot(e["t"], e["xs"][:n4] + e["akv"])
        for w in cur:
            d, q, sub = w
            e = st[w]
            sl = slice(q * qw, (q + 1) * qw)
            y_st = e["xs"][n4:] + _dot(e["a_rb"], e["u"]) + e["rkv"]
            ins[d][6][sub * c_len:(sub + 1) * c_len, sl] = _head_unstack(y_st)
            wc = ins[d][5][0, pl.ds((first_chunk[d] + sub) % per_tile, 1), sl]
            state[(d, q)] = (state[(d, q)] + _dot_tn(e["u"], e["ms_b"]) + e["vtk"]) * wc
    for (d, q) in chains:
        s_ref[d, :, q * qw:(q + 1) * qw] = state[(d, q)]


def _rwkv_scan(at, rt, bt, kt, wc, v, masks, same, *, nb, tb, ctx_len, feat_tm):
    r_tot = v.shape[0]
    gw = GROUP_W
    nsub = RWKV_STEP_CHUNKS
    c = nsub * RWKV_CHUNK
    per_tile = feat_tm // RWKV_CHUNK
    assert tb % c == 0 and ctx_len % c == 0 and per_tile % nsub == 0
    n_groups = tb // c
    ctx_groups = ctx_len // c

    def cf(b, i):
        return b * n_groups + i

    def cb(b, i):
        return b * n_groups + jnp.where(i < ctx_groups, ctx_groups - 1 - i, n_groups - 1 + ctx_groups - i)

    def specs(d, cidx):
        feat = pl.BlockSpec((1, c, gw), lambda b, i: (d, cidx(b, i), 0))
        return [feat, feat, feat, feat,
                pl.BlockSpec((c, gw), lambda b, i: (cidx(b, i), 0)),
                pl.BlockSpec((1, 8, gw), lambda b, i: (d, cidx(b, i) * nsub // per_tile, 0))]

    return pl.pallas_call(
        functools.partial(_rwkv_scan_kernel, n_groups=n_groups, ctx_groups=ctx_groups, per_tile=per_tile),
        grid=(nb, n_groups),
        in_specs=specs(0, cf) + specs(1, cb)
        + [pl.BlockSpec((2, RWKV_N_MASKS, RWKV_QUAD, RWKV_QUAD), lambda b, i: (0, 0, 0, 0)),
           pl.BlockSpec((RWKV_QUAD, RWKV_QUAD), lambda b, i: (0, 0))],
        out_specs=[pl.BlockSpec((c, gw), lambda b, i: (cf(b, i), 0)),
                   pl.BlockSpec((c, gw), lambda b, i: (cb(b, i), 0))],
        out_shape=[jax.ShapeDtypeStruct((r_tot, gw), F32), jax.ShapeDtypeStruct((r_tot, gw), F32)],
        scratch_shapes=[pltpu.VMEM((2, RWKV_HEAD, gw), F32)],
        compiler_params=_cparams("arbitrary", "arbitrary"),
    )(at, rt, bt, kt, v, wc, at, rt, bt, kt, v, wc, masks, same)


def _rwkv_finish_kernel(yf_ref, yb_ref, gate_ref, bonus_ref, vecs_ref, ones_ref, o_ref):
    ones = ones_ref[...]
    y = yf_ref[...] + yb_ref[...]
    inv = 1.0 / RWKV_HEAD
    mu = _dot_wide_lhs(y, ones) * inv
    yc = y - mu
    var = _dot_wide_lhs(yc * yc, ones) * inv
    out = yc * lax.rsqrt(var + RWKV_GN_EPS) * vecs_ref[3:4, :] + vecs_ref[4:5, :] + bonus_ref[...]
    o_ref[...] = (out * gate_ref[...]).astype(o_ref.dtype)


def _rwkv_finish(yf, yb, gate, bonus, vecs, ones_bd, *, nb, tb):
    r_tot = gate.shape[0]
    gw = GROUP_W
    tm = _pick_tile(tb, 768)
    nt = tb // tm
    row = lambda b, t: (b * nt + t, 0)
    return pl.pallas_call(
        _rwkv_finish_kernel,
        grid=(nb, nt),
        in_specs=[pl.BlockSpec((tm, gw), row), pl.BlockSpec((tm, gw), row),
                  pl.BlockSpec((tm, gw), row), pl.BlockSpec((tm, gw), row),
                  pl.BlockSpec((5, gw), lambda b, t: (0, 0)),
                  pl.BlockSpec((gw, gw), lambda b, t: (0, 0))],
        out_specs=pl.BlockSpec((tm, gw), row),
        out_shape=jax.ShapeDtypeStruct((r_tot, gw), BF16),
        compiler_params=_cparams("arbitrary", "arbitrary"),
    )(yf, yb, gate, bonus, vecs, ones_bd)


def _mla_proj_kernel(p_ref, qn_ref, kvn_ref, wq_ref, wk_ref, wv_ref, cos_ref, sa_ref, sb_ref,
                     q_ref, k_ref, v_ref):
    p = p_ref[...].astype(F32)
    c_q = p[:, :MLA_Q_RANK]
    c_kv = p[:, MLA_Q_RANK:MLA_Q_RANK + MLA_KV_RANK]
    k_r = p[:, MLA_Q_RANK + MLA_KV_RANK:]
    n_q = _rms(c_q, qn_ref[...])
    n_kv = _rms(c_kv, kvn_ref[...])
    q = _dot(n_q, wq_ref[...])
    kx = _dot(n_kv, wk_ref[...])
    v_ref[...] = _dot(n_kv, wv_ref[...]).astype(BF16)
    cos, sa, sb = cos_ref[...], sa_ref[...], sb_ref[...]
    kr_blk = jnp.concatenate([jnp.zeros_like(k_r), k_r], axis=1)
    hp = MLA_HEAD_PAD
    for h in range(MLA_HEADS):
        sl = slice(h * hp, (h + 1) * hp)
        q_ref[:, sl] = (_rope(q[:, sl], cos, sa, sb, MLA_ROPE // 4) * MLA_LOGIT_SCALE).astype(BF16)
        k_ref[:, sl] = _rope(kx[:, sl] + kr_blk, cos, sa, sb, MLA_ROPE // 4).astype(BF16)


def _mla_project(pb, qn, kvn, wq, wk, wv, cos, sa, sb, *, nb, tb):
    r_tot = pb.shape[0]
    tm = _pick_tile(tb, 768)
    nt = tb // tm
    row = lambda b, t: (b * nt + t, 0)
    trow = lambda b, t: (t, 0)
    c2 = lambda b, t: (0, 0)
    hw = MLA_HEADS * MLA_HEAD_PAD
    return pl.pallas_call(
        _mla_proj_kernel,
        grid=(nb, nt),
        in_specs=[pl.BlockSpec((tm, MLA_IN_PAD), lambda b, t: (b * nt + t, P_MLA // MLA_IN_PAD)),
                  pl.BlockSpec((1, MLA_Q_RANK), c2), pl.BlockSpec((1, MLA_KV_RANK), c2),
                  pl.BlockSpec((MLA_Q_RANK, hw), c2), pl.BlockSpec((MLA_KV_RANK, hw), c2),
                  pl.BlockSpec((MLA_KV_RANK, GROUP_W), c2),
                  pl.BlockSpec((tm, MLA_HEAD_PAD), trow), pl.BlockSpec((tm, MLA_HEAD_PAD), trow),
                  pl.BlockSpec((tm, MLA_HEAD_PAD), trow)],
        out_specs=[pl.BlockSpec((tm, hw), row), pl.BlockSpec((tm, hw), row), pl.BlockSpec((tm, GROUP_W), row)],
        out_shape=[jax.ShapeDtypeStruct((r_tot, hw), BF16), jax.ShapeDtypeStruct((r_tot, hw), BF16),
                   jax.ShapeDtypeStruct((r_tot, GROUP_W), BF16)],
        compiler_params=_cparams("arbitrary", "arbitrary"),
    )(pb, qn, kvn, wq, wk, wv, cos, sa, sb)


def _softmax_pv(s, v):
    m = jnp.max(s, axis=-1, keepdims=True)
    e = jnp.exp2(s - m)
    l = jnp.sum(e, axis=-1, keepdims=True)
    return _dot(e, v) / l


def _mla_attn_kernel(q_ref, k_ref, v_ref, o_ref, *, ctx_len, tq, n_qt):
    k_all = k_ref[...]
    v_all = v_ref[...]
    o_ref[0:ctx_len, :] = _softmax_pv(_dot_nt(q_ref[0:ctx_len, :], k_all[0:ctx_len]), v_all[0:ctx_len]).astype(BF16)

    for i in range(n_qt):
        rows = slice(ctx_len + i * tq, ctx_len + (i + 1) * tq)
        o_ref[rows, :] = _softmax_pv(_dot_nt(q_ref[rows, :], k_all), v_all).astype(BF16)


def _mla_attention(q, k, v, *, nb, tb, ctx_len):
    r_tot = q.shape[0]
    tq = 256
    assert (tb - ctx_len) % tq == 0 and ctx_len % tq == 0
    hp = MLA_HEAD_PAD
    return pl.pallas_call(
        functools.partial(_mla_attn_kernel, ctx_len=ctx_len, tq=tq, n_qt=(tb - ctx_len) // tq),
        grid=(nb, MLA_HEADS),
        in_specs=[pl.BlockSpec((tb, hp), lambda b, h: (b, h)),
                  pl.BlockSpec((tb, hp), lambda b, h: (b, h)),
                  pl.BlockSpec((tb, MLA_V), lambda b, h: (b, h))],
        out_specs=pl.BlockSpec((tb, MLA_V), lambda b, h: (b, h)),
        out_shape=jax.ShapeDtypeStruct((r_tot, GROUP_W), BF16),
        compiler_params=_cparams("arbitrary", "arbitrary"),
    )(q, k, v)


def _seq_edges(n, ctx_len):
    rows = lax.broadcasted_iota(jnp.int32, (n, 1), 0)
    starts = jnp.logical_or(rows == 0, rows == ctx_len)
    ends = jnp.logical_or(rows == ctx_len - 1, rows == n - 1)
    return starts, ends


def _dwconv3_rows(u, w_ref, starts, ends):
    n = u.shape[0]
    prev = jnp.where(starts, 0.0, pltpu.roll(u, 1, 0))
    nxt = jnp.where(ends, 0.0, pltpu.roll(u, n - 1, 0))
    return prev * w_ref[0:1, :] + u * w_ref[1:2, :] + nxt * w_ref[2:3, :]


def _conv_mix_kernel(b_ref, c_ref, u_ref, w_ref, o_ref, *, ctx_len):
    starts, ends = _seq_edges(b_ref.shape[0], ctx_len)
    z = c_ref[...].astype(F32) * u_ref[...].astype(F32)
    o_ref[...] = (b_ref[...].astype(F32) * _dwconv3_rows(z, w_ref, starts, ends)).astype(o_ref.dtype)


def _conv_mixer(pc, conv_w, *, nb, tb, ctx_len):
    r_tot = pc.shape[0]
    tc = 256
    nj = GROUP_W // tc
    return pl.pallas_call(
        functools.partial(_conv_mix_kernel, ctx_len=ctx_len),
        grid=(nb, nj),
        in_specs=[pl.BlockSpec((tb, tc), lambda b, j: (b, P_CONV_B // tc + j)),
                  pl.BlockSpec((tb, tc), lambda b, j: (b, P_CONV_C // tc + j)),
                  pl.BlockSpec((tb, tc), lambda b, j: (b, P_CONV_U // tc + j)),
                  pl.BlockSpec((3, tc), lambda b, j: (0, j))],
        out_specs=pl.BlockSpec((tb, tc), lambda b, j: (b, j)),
        out_shape=jax.ShapeDtypeStruct((r_tot, GROUP_W), BF16),
        compiler_params=_cparams("arbitrary", "arbitrary"),
    )(pc, pc, pc, conv_w)


def _ret_kernel(q_ref, k_ref, v_ref, g_ref, dec_ref, gn_ref, cos_ref, sa_ref, sb_ref, o_ref,
                qs_ref, ks_ref, vs_ref, dm_ref, *, ctx_len, seq_len):
    blk = ctx_len
    nlat = seq_len // blk
    tb = ctx_len + seq_len
    cos, sa, sb = cos_ref[...], sa_ref[...], sb_ref[...]
    qs_ref[...] = _rope(q_ref[...].astype(F32), cos, sa, sb, RET_HEAD // 4).astype(BF16)
    kr = _rope(k_ref[...].astype(F32), cos, sa, sb, RET_HEAD // 4).astype(BF16)
    ks_ref[0:tb, :] = kr
    ks_ref[tb:tb + blk, :] = kr[0:blk]
    vs_ref[0:tb, :] = v_ref[...]
    vs_ref[tb:tb + blk, :] = v_ref[0:blk, :]
    lgf = -jnp.exp(dec_ref[0, 0:1, 0:1])
    lgb = -jnp.exp(dec_ref[0, 1:2, 0:1])
    gn = gn_ref[...]
    scale = RET_HEAD ** -0.5
    rc = (lax.broadcasted_iota(jnp.int32, (blk, blk), 0) - lax.broadcasted_iota(jnp.int32, (blk, blk), 1)).astype(F32)
    dm_ref[0] = jnp.exp(jnp.where(rc >= 0, rc * lgf, -rc * lgb))
    for dist in range(1, nlat + 1):
        dm_ref[dist] = jnp.exp((dist * blk + rc) * lgf)
        dm_ref[nlat + dist] = jnp.exp((dist * blk - rc) * lgb)

    def finish(o, g):
        mu = jnp.mean(o, axis=-1, keepdims=True)
        oc = o - mu
        var = jnp.mean(oc * oc, axis=-1, keepdims=True)
        return _silu(g) * (oc * lax.rsqrt(var + GN_EPS) * gn)

    s_c = _dot_nt(qs_ref[0:blk, :], ks_ref[0:blk, :]) * dm_ref[0]
    o_c = _dot(s_c, vs_ref[0:blk, :]) * scale
    o_ref[0:blk, :] = finish(o_c, g_ref[0:blk, :].astype(F32)).astype(o_ref.dtype)

    for i in range(nlat):
        rows = slice(ctx_len + i * blk, ctx_len + (i + 1) * blk)
        s = _dot_nt(qs_ref[rows, :], ks_ref[...])
        parts = []
        for jp in range(nlat + 2):
            j = jp - 1
            m = dm_ref[i - j] if j < i else (dm_ref[0] if j == i else dm_ref[nlat + j - i])
            parts.append((s[:, jp * blk:(jp + 1) * blk] * m).astype(BF16))
        o = jnp.dot(jnp.concatenate(parts, axis=1), vs_ref[...], preferred_element_type=F32) * scale
        o_ref[rows, :] = finish(o, g_ref[rows, :].astype(F32)).astype(o_ref.dtype)


def _retention(pd, dec, gn, cos, sa, sb, *, nb, tb, ctx_len):
    r_tot = pd.shape[0]
    hd = RET_HEAD
    nh = RET_HEADS
    seq_len = tb - ctx_len
    assert seq_len % ctx_len == 0
    nlat = seq_len // ctx_len
    tbl = pl.BlockSpec((tb, hd), lambda b, h: (0, 0))
    return pl.pallas_call(
        functools.partial(_ret_kernel, ctx_len=ctx_len, seq_len=seq_len),
        grid=(nb, nh),
        in_specs=[pl.BlockSpec((tb, hd), lambda b, h: (b, P_RET // hd + h)),
                  pl.BlockSpec((tb, hd), lambda b, h: (b, P_RET // hd + nh + h)),
                  pl.BlockSpec((tb, hd), lambda b, h: (b, P_RET // hd + 2 * nh + h)),
                  pl.BlockSpec((tb, hd), lambda b, h: (b, P_RET // hd + 3 * nh + h)),
                  pl.BlockSpec((1, 8, 128), lambda b, h: (h, 0, 0)),
                  pl.BlockSpec((1, hd), lambda b, h: (0, h)),
                  tbl, tbl, tbl],
        out_specs=pl.BlockSpec((tb, hd), lambda b, h: (b, h)),
        out_shape=jax.ShapeDtypeStruct((r_tot, GROUP_W), BF16),
        scratch_shapes=[pltpu.VMEM((tb, hd), BF16), pltpu.VMEM((tb + ctx_len, hd), BF16),
                        pltpu.VMEM((tb + ctx_len, hd), BF16), pltpu.VMEM((2 * nlat + 1, ctx_len, ctx_len), F32)],
        compiler_params=_cparams("arbitrary", "arbitrary"),
    )(pd, pd, pd, pd, dec, gn, cos, sa, sb)


def _residual_epilogue(y, x_ref, mc_ref, mb_ref, g_ref, is_ctx, x_out_ref, h_out_ref):
    def rows(k, scale_by=None, plus_one=False):
        r_ctx, r_lat = mc_ref[k:k + 1, :], mb_ref[0, k:k + 1, :]
        if plus_one:
            r_ctx, r_lat = 1.0 + r_ctx, 1.0 + r_lat
        if scale_by is not None:
            r_ctx, r_lat = r_ctx * scale_by, r_lat * scale_by
        return jnp.where(is_ctx, r_ctx, r_lat)

    inv_y = lax.rsqrt(jnp.mean(y * y, axis=-1, keepdims=True) + NORM_EPS)
    x_new = x_ref[...] + (y * inv_y) * rows(0, scale_by=g_ref[0:1, :])
    x_out_ref[...] = x_new
    inv_x = lax.rsqrt(jnp.mean(x_new * x_new, axis=-1, keepdims=True) + NORM_EPS)
    h = (x_new * inv_x) * rows(2, scale_by=g_ref[1:2, :], plus_one=True) + rows(1)
    h_out_ref[...] = h.astype(BF16)


def _out_proj_kernel(ya_ref, yb_ref, yc_ref, yd_ref, w_ref, x_ref, mc_ref, mb_ref, g_ref, x_out_ref, h_out_ref,
                     *, tm, ctx_len):
    gw = GROUP_W
    y = (jnp.dot(ya_ref[...], w_ref[0:gw, :], preferred_element_type=F32)
         + jnp.dot(yb_ref[...], w_ref[gw:2 * gw, :], preferred_element_type=F32)
         + jnp.dot(yc_ref[...], w_ref[2 * gw:3 * gw, :], preferred_element_type=F32)
         + jnp.dot(yd_ref[...], w_ref[3 * gw:, :], preferred_element_type=F32))
    is_ctx = _ctx_rows(pl.program_id(1), tm, ctx_len)
    _residual_epilogue(y, x_ref, mc_ref, mb_ref, g_ref, is_ctx, x_out_ref, h_out_ref)


def _out_proj(ya, yb, yc, yd, w, x, mc3, mb3, g2, *, nb, tb, ctx_len):
    d = x.shape[1]
    gw = GROUP_W
    tm = _pick_tile(tb, 768)
    nt = tb // tm
    row = lambda b, t: (b * nt + t, 0)
    c2 = lambda b, t: (0, 0)
    return pl.pallas_call(
        functools.partial(_out_proj_kernel, tm=tm, ctx_len=ctx_len),
        grid=(nb, nt),
        in_specs=[pl.BlockSpec((tm, gw), row)] * 4
        + [pl.BlockSpec((4 * gw, d), c2, pipeline_mode=pl.Buffered(1)), pl.BlockSpec((tm, d), row),
           pl.BlockSpec((3, d), c2), pl.BlockSpec((1, 3, d), lambda b, t: (b, 0, 0)), pl.BlockSpec((2, d), c2)],
        out_specs=[pl.BlockSpec((tm, d), row), pl.BlockSpec((tm, d), row)],
        out_shape=[jax.ShapeDtypeStruct(x.shape, F32), jax.ShapeDtypeStruct(x.shape, BF16)],
        compiler_params=_cparams("arbitrary", "arbitrary"),
    )(ya, yb, yc, yd, w, x, mc3, mb3, g2)


FFN_PAD = 8


FFN_ROW_CHUNK = 768


def _ffn_up_kernel(h_ref, wg_ref, wv_ref, cg_ref, cv_ref, o_ref, ug_ref, uv_ref, *, ctx_len):
    n, tn = o_ref.shape
    rc, pad = FFN_ROW_CHUNK, FFN_PAD
    nch = n // rc
    wg = wg_ref[...].astype(BF16)
    wv = wv_ref[...].astype(BF16)
    for ref in (ug_ref, uv_ref):
        ref[0:pad, :] = jnp.zeros((pad, tn), F32)
        ref[pad + n:2 * pad + n, :] = jnp.zeros((pad, tn), F32)
    local = lax.broadcasted_iota(jnp.int32, (rc, 1), 0)

    def matmuls(c):
        h = h_ref[c * rc:(c + 1) * rc, :]
        ug_ref[pad + c * rc:pad + (c + 1) * rc, :] = jnp.dot(h, wg, preferred_element_type=F32)
        uv_ref[pad + c * rc:pad + (c + 1) * rc, :] = jnp.dot(h, wv, preferred_element_type=F32)

    def conv(ref, w_ref, c):
        r0 = c * rc
        prev = ref[pad + r0 - 1:pad + r0 - 1 + rc, :]
        cur = ref[pad + r0:pad + r0 + rc, :]
        nxt = ref[pad + r0 + 1:pad + r0 + 1 + rc, :]
        if r0 <= ctx_len < r0 + rc:
            prev = jnp.where(local == ctx_len - r0, 0.0, prev)
        if r0 <= ctx_len - 1 < r0 + rc:
            nxt = jnp.where(local == ctx_len - 1 - r0, 0.0, nxt)
        return prev * w_ref[0:1, :] + cur * w_ref[1:2, :] + nxt * w_ref[2:3, :]

    def epilogue(c):
        gate = conv(ug_ref, cg_ref, c)
        val = conv(uv_ref, cv_ref, c)
        o_ref[c * rc:(c + 1) * rc, :] = (_silu(gate) * val).astype(o_ref.dtype)

    for c in range(nch + 2):
        if c < nch:
            matmuls(c)
        if c >= 2:
            epilogue(c - 2)


def _ffn_up(h, w_up, w_conv, layer, *, nb, tb, ctx_len):
    r_tot, d = h.shape
    dff = w_up.shape[2] // 2
    tn = 512
    nj = dff // tn
    assert tb % FFN_ROW_CHUNK == 0
    buf = pltpu.VMEM((tb + 2 * FFN_PAD, tn), F32)
    return pl.pallas_call(
        functools.partial(_ffn_up_kernel, ctx_len=ctx_len),
        grid=(nb, nj),
        in_specs=[pl.BlockSpec((tb, d), lambda b, j: (b, 0), pipeline_mode=pl.Buffered(1)),
                  pl.BlockSpec((None, d, tn), lambda b, j: (layer, 0, j)),
                  pl.BlockSpec((None, d, tn), lambda b, j: (layer, 0, nj + j)),
                  pl.BlockSpec((None, 3, tn), lambda b, j: (layer, 0, j)),
                  pl.BlockSpec((None, 3, tn), lambda b, j: (layer, 0, nj + j))],
        out_specs=pl.BlockSpec((tb, tn), lambda b, j: (b, j)),
        out_shape=jax.ShapeDtypeStruct((r_tot, dff), BF16),
        scratch_shapes=[buf, buf],
        compiler_params=_cparams("arbitrary", "arbitrary"),
    )(h, w_up, w_up, w_conv, w_conv)


def _ffn_down_kernel(a_ref, w_ref, x_ref, mc_ref, mb_ref, g_ref, x_out_ref, h_out_ref, *, tm, ctx_len):
    y = jnp.dot(a_ref[...], w_ref[...], preferred_element_type=F32)
    is_ctx = _ctx_rows(pl.program_id(1), tm, ctx_len)
    _residual_epilogue(y, x_ref, mc_ref, mb_ref, g_ref, is_ctx, x_out_ref, h_out_ref)


def _ffn_down(act, w, x, mc3, mb3, g2, *, nb, tb, ctx_len):
    d = x.shape[1]
    dff = act.shape[1]
    tm = _pick_tile(tb, 384)
    nt = tb // tm
    row = lambda b, t: (b * nt + t, 0)
    c2 = lambda b, t: (0, 0)
    return pl.pallas_call(
        functools.partial(_ffn_down_kernel, tm=tm, ctx_len=ctx_len),
        grid=(nb, nt),
        in_specs=[pl.BlockSpec((tm, dff), row),
                  pl.BlockSpec((dff, d), c2, pipeline_mode=pl.Buffered(1)),
                  pl.BlockSpec((tm, d), row),
                  pl.BlockSpec((3, d), c2), pl.BlockSpec((1, 3, d), lambda b, t: (b, 0, 0)), pl.BlockSpec((2, d), c2)],
        out_specs=[pl.BlockSpec((tm, d), row), pl.BlockSpec((tm, d), row)],
        out_shape=[jax.ShapeDtypeStruct(x.shape, F32), jax.ShapeDtypeStruct(x.shape, BF16)],
        compiler_params=_cparams("arbitrary", "arbitrary"),
    )(act, w, x, mc3, mb3, g2)


def _pad_cols(w, n):
    return jnp.pad(w, ((0, 0), (0, n - w.shape[1])))


def _layout_w_in(w_in):
    gw = GROUP_W
    a = w_in[:, :RWKV_IN]
    o = 3 * gw
    wa = jnp.concatenate([a[:, :o],
                          _pad_cols(a[:, o:o + RWKV_DECAY_LORA], RWKV_LORA_PAD),
                          _pad_cols(a[:, o + RWKV_DECAY_LORA:o + RWKV_DECAY_LORA + RWKV_ICLR_LORA], RWKV_LORA_PAD),
                          a[:, o + RWKV_DECAY_LORA + RWKV_ICLR_LORA:]], axis=1)
    wb = _pad_cols(w_in[:, RWKV_IN:RWKV_IN + MLA_IN], MLA_IN_PAD)
    wc = w_in[:, RWKV_IN + MLA_IN:RWKV_IN + MLA_IN + CONV_IN]
    wd = w_in[:, RWKV_IN + MLA_IN + CONV_IN:]
    w = jnp.concatenate([wa, wd, wc[:, :gw], wb, wc[:, gw:]], axis=1)
    assert w.shape[1] == P_WIDTH
    return w.astype(BF16)


def _layout_rwkv_shift(shift):
    o = 3 * GROUP_W
    return jnp.concatenate([shift[:, :o],
                            _pad_cols(shift[:, o:o + RWKV_DECAY_LORA], RWKV_LORA_PAD),
                            _pad_cols(shift[:, o + RWKV_DECAY_LORA:o + RWKV_DECAY_LORA + RWKV_ICLR_LORA], RWKV_LORA_PAD),
                            shift[:, o + RWKV_DECAY_LORA + RWKV_ICLR_LORA:]], axis=1)


def _pad_lora_rows(w):
    return jnp.pad(w, ((0, 0), (0, RWKV_LORA_PAD - w.shape[1]), (0, 0))).astype(BF16)


def _layout_mla(w_uq, w_ukv):
    hp = MLA_HEAD_PAD
    dqk = MLA_NOPE + MLA_ROPE
    wq = jnp.concatenate([_pad_cols(w_uq[:, h * dqk:(h + 1) * dqk], hp) for h in range(MLA_HEADS)], axis=1)
    dkv = MLA_NOPE + MLA_V
    wk = jnp.concatenate([_pad_cols(w_ukv[:, h * dkv:h * dkv + MLA_NOPE], hp) for h in range(MLA_HEADS)], axis=1)
    wv = jnp.concatenate([w_ukv[:, h * dkv + MLA_NOPE:(h + 1) * dkv] for h in range(MLA_HEADS)], axis=1)
    return wq.astype(BF16), wk.astype(BF16), wv.astype(BF16)


def _rope_tables(seq_len, ctx_len, rot_dim, lead, width):
    rows = seq_len // GRID_W
    half = rot_dim // 2
    quarter = half // 2
    inv = ROPE_BASE ** (-jnp.arange(0, half, 2, dtype=F32) / half)
    row = jnp.repeat(jnp.arange(rows, dtype=F32), GRID_W)
    col = jnp.tile(jnp.arange(GRID_W, dtype=F32), rows)
    zeros = jnp.zeros((seq_len, quarter), F32)
    cos_parts, sa_parts, sb_parts = [], [], []
    for pos in (row, col):
        ang = pos[:, None] * inv[None, :]
        c, s = jnp.cos(ang), jnp.sin(ang)
        cos_parts += [c, c]
        sa_parts += [-s, zeros]
        sb_parts += [zeros, s]

    def table(parts, fill):
        body = jnp.concatenate(parts, axis=1)
        body = jnp.concatenate([jnp.full((seq_len, lead), fill, F32), body,
                                jnp.full((seq_len, width - lead - rot_dim), fill, F32)], axis=1)
        return jnp.concatenate([jnp.full((ctx_len, width), fill, F32), body], axis=0)

    return table(cos_parts, 1.0), table(sa_parts, 0.0), table(sb_parts, 0.0)


def _head_sum_matrix(width, head):
    i = jnp.arange(width) // head
    return (i[:, None] == i[None, :]).astype(BF16)


def _chunk_tri(tm, chunk):
    i = jnp.arange(tm)
    same = (i[:, None] // chunk) == (i[None, :] // chunk)
    lower = jnp.logical_and(same, i[None, :] <= i[:, None])
    upper = jnp.logical_and(same, i[None, :] >= i[:, None])
    return jnp.stack([lower, upper]).astype(BF16)


def _token_mixer(h, lw, tables, *, nb, tb, ctx_len):
    kw = dict(nb=nb, tb=tb)
    p = _matmul(h, lw["w_in"], BF16, tn=1280, **kw)
    pa = pb = pc = pd = p

    at, rt, bt, kt, wcum, v, gate, bonus = _rwkv_features(
        pa, lw["rwkv_shift"], lw["rwkv_w0"], lw["rwkv_w_up"], lw["rwkv_a0"], lw["rwkv_a_up"], lw["rwkv_g_up"],
        lw["rwkv_vecs"], tables["ones_bd"], tables["tri"], ctx_len=ctx_len, **kw)
    yf, yb_ = _rwkv_scan(at, rt, bt, kt, wcum, v, tables["scan_masks"], tables["same_head"],
                         ctx_len=ctx_len, feat_tm=256, **kw)
    ya = _rwkv_finish(yf, yb_, gate, bonus, lw["rwkv_vecs"], tables["ones_bd"], **kw)

    wq, wk, wv = lw["mla_w"]
    q, k, vv = _mla_project(pb, lw["mla_q_norm"], lw["mla_kv_norm"], wq, wk, wv, *tables["mla_rope"], **kw)
    yb = _mla_attention(q, k, vv, ctx_len=ctx_len, **kw)

    yc = _conv_mixer(pc, lw["conv_w"], ctx_len=ctx_len, **kw)
    yd = _retention(pd, lw["ret_decay"], lw["ret_gn_g"], *tables["ret_rope"], ctx_len=ctx_len, **kw)
    return ya, yb, yc, yd


def kernel(x, c, ctx, c_ctx, mod_w, mod_b, norm_g, w_in, rwkv_shift, rwkv_w0, rwkv_w_up, rwkv_a0, rwkv_a_up,
           rwkv_g_up, rwkv_vecs, mla_q_norm, mla_kv_norm, mla_w_uq, mla_w_ukv, conv_w, ret_decay, ret_gn_g, w_out,
           mlp_w_up, mlp_conv, mlp_w_down):
    nb, seq_len, d = x.shape
    ctx_len = ctx.shape[1]
    depth = mod_w.shape[0]
    tb = ctx_len + seq_len
    assert nb + 1 <= 8 and d == D_MODEL
    kw = dict(nb=nb, tb=tb, ctx_len=ctx_len)

    tables = {
        "ones_bd": _head_sum_matrix(GROUP_W, RWKV_HEAD),
        "tri": _chunk_tri(256, RWKV_CHUNK),
        "scan_masks": _rwkv_scan_masks(),
        "same_head": _head_sum_matrix(RWKV_QUAD, RWKV_HEAD),
        "mla_rope": _rope_tables(seq_len, ctx_len, MLA_ROPE, MLA_NOPE, MLA_HEAD_PAD),
        "ret_rope": _rope_tables(seq_len, ctx_len, RET_HEAD, 0, RET_HEAD),
    }

    c_pad = jnp.concatenate([c, c_ctx[None, :], jnp.zeros((8 - nb - 1, d), F32)], axis=0)
    mods = _modulation(c_pad, mod_w, mod_b).reshape(depth, 8, N_MOD, d)
    m_lat = mods[:, :nb]
    m_ctx = mods[:, nb]

    xs, h = _prologue(ctx, x, m_ctx[0], m_lat[0], norm_g[0, 0:1], **kw)

    for l in range(depth):
        lw = {
            "w_in": _layout_w_in(w_in[l]),
            "rwkv_shift": _layout_rwkv_shift(rwkv_shift[l]),
            "rwkv_w0": rwkv_w0[l], "rwkv_w_up": _pad_lora_rows(rwkv_w_up[l]),
            "rwkv_a0": rwkv_a0[l], "rwkv_a_up": _pad_lora_rows(rwkv_a_up[l]),
            "rwkv_g_up": rwkv_g_up[l].astype(BF16), "rwkv_vecs": rwkv_vecs[l],
            "mla_q_norm": mla_q_norm[l][None, :], "mla_kv_norm": mla_kv_norm[l][None, :],
            "mla_w": _layout_mla(mla_w_uq[l], mla_w_ukv[l]),
            "conv_w": conv_w[l],
            "ret_decay": jnp.broadcast_to(
                jnp.pad(ret_decay[l].T, ((0, 0), (0, 6)))[:, :, None], (RET_HEADS, 8, 128)),
            "ret_gn_g": ret_gn_g[l][None, :],
        }
        ya, yb, yc, yd = _token_mixer(h, lw, tables, **kw)

        xs, h = _out_proj(ya, yb, yc, yd, w_out[l].astype(BF16), xs,
                          m_ctx[l][jnp.array([2, 3, 4])], m_lat[l][:, jnp.array([2, 3, 4])],
                          norm_g[l, 1:3], **kw)
        act = _ffn_up(h, mlp_w_up, mlp_conv, l, **kw)
        nl = min(l + 1, depth - 1)
        mc3 = jnp.stack([m_ctx[l][5], m_ctx[nl][0], m_ctx[nl][1]])
        mb3 = jnp.stack([m_lat[l][:, 5], m_lat[nl][:, 0], m_lat[nl][:, 1]], axis=1)
        g2 = jnp.stack([norm_g[l, 3], norm_g[nl, 0]])
        xs, h = _ffn_down(act, mlp_w_down[l].astype(BF16), xs, mc3, mb3, g2, **kw)

    return xs.reshape(nb, tb, d)[:, ctx_len:]
```
